```python
import numpy as np
import jax
import jax.numpy as jnp
from jax import lax

D_MODEL = 4096
BATCH = 8
SEQ = 2048
DEPTH = 4

HEAD_DIM = 128
NSA_HEADS = 16
NSA_KV_GROUPS = 4
NSA_HPG = NSA_HEADS // NSA_KV_GROUPS
CMP_BLOCK = 32
CMP_STRIDE = 16
CMP_HIDDEN = 128
SLC_BLOCK = 64
SLC_TOPK = 16
WINDOW = 512
ATTN_Q_BLOCK = 128
SLC_Q_BLOCK = 16
ROPE_THETA = 500000.0
ROPE_DIM = HEAD_DIM // 4
GLA_HEADS = 4
GLA_DK = 256
GLA_DV = 512
GLA_RANK = 16
GLA_TAU = 16.0
GLA_CHUNK = 64
CONV_WIDTH = 31
FFN_HIDDEN = (8 * D_MODEL + 3 * 256 - 1) // (3 * 256) * 256
N_MOD = 6
N_EVEN = (DEPTH + 1) // 2
N_ODD = DEPTH // 2

NSA_Q = NSA_HEADS * HEAD_DIM
NSA_KV = NSA_KV_GROUPS * HEAD_DIM
EVEN_IN_SPLITS = (NSA_Q, NSA_KV, NSA_KV, NSA_KV, NSA_KV, NSA_KV, NSA_KV, NSA_HEADS * 3,
                  GLA_HEADS * GLA_DK, GLA_HEADS * GLA_DK, GLA_HEADS * GLA_DV, GLA_RANK, GLA_HEADS * GLA_DV)
EVEN_IN = sum(EVEN_IN_SPLITS)
MIX_OUT = NSA_Q + GLA_HEADS * GLA_DV

kernel_name = 'hybrid_nsa_gla_conformer_adaln'


def rms_norm(x, g, eps=1e-6):
    xf = x.astype(jnp.float32)
    y = xf * lax.rsqrt(jnp.mean(xf * xf, axis=-1, keepdims=True) + eps)
    return (y * g.astype(jnp.float32)).astype(x.dtype)


def layer_norm(x, g, b, eps=1e-5):
    xf = x.astype(jnp.float32)
    mu = jnp.mean(xf, axis=-1, keepdims=True)
    var = jnp.mean(jnp.square(xf - mu), axis=-1, keepdims=True)
    y = (xf - mu) * lax.rsqrt(var + eps)
    return (y * g.astype(jnp.float32) + b.astype(jnp.float32)).astype(x.dtype)


def partial_rope(x, pos):
    half = ROPE_DIM // 2
    inv_freq = ROPE_THETA ** (-jnp.arange(half, dtype=jnp.float32) / half)
    ang = pos.astype(jnp.float32)[:, None] * inv_freq[None, :]
    cos, sin = jnp.cos(ang), jnp.sin(ang)
    xf = x.astype(jnp.float32)
    x1, x2 = xf[..., :half], xf[..., half:ROPE_DIM]
    out = jnp.concatenate([x1 * cos - x2 * sin, x2 * cos + x1 * sin, xf[..., ROPE_DIM:]], axis=-1)
    return out.astype(x.dtype)


def masked_softmax(s, mask):
    s = jnp.where(mask, s.astype(jnp.float32), -1e30)
    return jnp.where(mask, jax.nn.softmax(s, axis=-1), 0.0)


def nsa_mixer(q, kc, vc, ks, vs, kw, vw, gate_logits, q_norm_g, k_norm_g, cmp_pos, cmp_w1, cmp_w2):
    B, S, _ = q.shape
    G, HPG, DH = NSA_KV_GROUPS, NSA_HPG, HEAD_DIM
    t = jnp.arange(S, dtype=jnp.int32)
    dt = q.dtype

    q = rms_norm(q.reshape(B, S, NSA_HEADS, DH), q_norm_g)
    q = q.reshape(B, S, G, HPG, DH).transpose(0, 2, 3, 1, 4)
    q = partial_rope(q, t) * (DH ** -0.5)

    def kv_heads(a):
        return a.reshape(B, S, G, DH).transpose(0, 2, 1, 3)
    kc, vc, ks, vs, kw, vw = map(kv_heads, (kc, vc, ks, vs, kw, vw))

    n_cmp = (S - CMP_BLOCK) // CMP_STRIDE + 1
    cmp_start = np.arange(n_cmp) * CMP_STRIDE
    blk_tok = cmp_start[:, None] + np.arange(CMP_BLOCK)[None, :]

    def compress(tok, j):
        blocks = tok[:, :, blk_tok] + cmp_pos[j]
        flat = blocks.reshape(B, G, n_cmp, CMP_BLOCK * DH)
        return jax.nn.gelu(flat @ cmp_w1[j]) @ cmp_w2[j]

    cmp_end = jnp.asarray(cmp_start + CMP_BLOCK - 1, dtype=jnp.int32)
    k_cmp = partial_rope(rms_norm(compress(kc, 0), k_norm_g[0]), cmp_end)
    v_cmp = compress(vc, 1)
    s_cmp = jnp.einsum('bghsd,bgnd->bghsn', q, k_cmp)
    p_cmp = masked_softmax(s_cmp, cmp_end[None, :] <= t[:, None])
    o_cmp = jnp.einsum('bghsn,bgnd->bghsd', p_cmp.astype(dt), v_cmp)

    n_slc = S // SLC_BLOCK
    n_top = min(SLC_TOPK, n_slc)
    slc_start = np.arange(n_slc) * SLC_BLOCK
    overlap = np.clip(np.minimum(cmp_start[:, None] + CMP_BLOCK, slc_start[None, :] + SLC_BLOCK)
                      - np.maximum(cmp_start[:, None], slc_start[None, :]), 0, None) / CMP_STRIDE
    imp = jnp.einsum('bghsn,nj->bgsj', p_cmp, jnp.asarray(overlap, dtype=jnp.float32))
    blk = jnp.arange(n_slc, dtype=jnp.int32)[None, :]
    cur = (t // SLC_BLOCK)[:, None]
    forced = (blk == 0) | (blk == cur) | (blk == cur - 1)
    valid = blk * SLC_BLOCK <= t[:, None]
    imp = jnp.where(forced, jnp.inf, jnp.where(valid, imp, -jnp.inf))
    top_val, top_idx = lax.top_k(imp, n_top)
    top_ok = top_val > -jnp.inf

    k_sb = rms_norm(ks, k_norm_g[1])
    k_sb = partial_rope(k_sb, t).reshape(B, G, n_slc, SLC_BLOCK, DH)
    v_sb = vs.reshape(B, G, n_slc, SLC_BLOCK, DH)
    nqs = S // SLC_Q_BLOCK
    q_s = jnp.moveaxis(q.reshape(B, G, HPG, nqs, SLC_Q_BLOCK, DH), 3, 0)
    idx_s = jnp.moveaxis(top_idx.reshape(B, G, nqs, SLC_Q_BLOCK, n_top), 2, 0)
    ok_s = jnp.moveaxis(top_ok.reshape(B, G, nqs, SLC_Q_BLOCK, n_top), 2, 0)
    t_s = t.reshape(nqs, SLC_Q_BLOCK)
    b_ix = jnp.arange(B)[:, None, None, None]
    g_ix = jnp.arange(G)[None, :, None, None]

    def slc_block(args):
        qb, idx, ok, tq = args
        kg = k_sb[b_ix, g_ix, idx]
        vg = v_sb[b_ix, g_ix, idx]
        s = jnp.einsum('bghqd,bgqnkd->bghqnk', qb, kg)
        kpos = idx[..., None] * SLC_BLOCK + jnp.arange(SLC_BLOCK, dtype=jnp.int32)
        m = (ok[..., None] & (kpos <= tq[:, None, None]))[:, :, None]
        shp = s.shape
        p = masked_softmax(s.reshape(shp[:4] + (-1,)),
                           jnp.broadcast_to(m, shp).reshape(shp[:4] + (-1,))).reshape(shp)
        return jnp.einsum('bghqnk,bgqnkd->bghqd', p.astype(dt), vg)

    o_slc = lax.map(slc_block, (q_s, idx_s, ok_s, t_s))
    o_slc = jnp.moveaxis(o_slc, 0, 3).reshape(B, G, HPG, S, DH)

    k_w = partial_rope(rms_norm(kw, k_norm_g[2]), t)
    pad = ((0, 0), (0, 0), (WINDOW, 0), (0, 0))
    k_pad, v_pad = jnp.pad(k_w, pad), jnp.pad(vw, pad)
    nqw = S // ATTN_Q_BLOCK
    span = ATTN_Q_BLOCK + WINDOW
    q_w = jnp.moveaxis(q.reshape(B, G, HPG, nqw, ATTN_Q_BLOCK, DH), 3, 0)

    def win_block(args):
        i, qb = args
        start = i * ATTN_Q_BLOCK
        kb = lax.dynamic_slice_in_dim(k_pad, start, span, axis=2)
        vb = lax.dynamic_slice_in_dim(v_pad, start, span, axis=2)
        tq = start + jnp.arange(ATTN_Q_BLOCK, dtype=jnp.int32)
        kp = start - WINDOW + jnp.arange(span, dtype=jnp.int32)
        m = (kp[None, :] >= 0) & (kp[None, :] <= tq[:, None]) & (tq[:, None] - kp[None, :] < WINDOW)
        s = jnp.einsum('bghqd,bgkd->bghqk', qb, kb)
        p = masked_softmax(s, m)
        return jnp.einsum('bghqk,bgkd->bghqd', p.astype(dt), vb)

    o_win = lax.map(win_block, (jnp.arange(nqw, dtype=jnp.int32), q_w))
    o_win = jnp.moveaxis(o_win, 0, 3).reshape(B, G, HPG, S, DH)

    g = jax.nn.sigmoid(gate_logits.astype(jnp.float32)).reshape(B, S, G, HPG, 3).transpose(0, 2, 3, 1, 4)
    o = g[..., 0:1] * o_cmp + g[..., 1:2] * o_slc + g[..., 2:3] * o_win
    return o.transpose(0, 3, 1, 2, 4).reshape(B, S, NSA_Q).astype(dt)


def gla_mixer(q, k, v, a_low, r, w_a2, b_a, norm_g):
    B, S, _ = q.shape
    nC = S // GLA_CHUNK
    f32 = jnp.float32

    def heads(a, d):
        return a.astype(f32).reshape(B, nC, GLA_CHUNK, GLA_HEADS, d).transpose(1, 0, 3, 2, 4)

    log_alpha = jax.nn.log_sigmoid((a_low @ w_a2 + b_a).astype(f32)) / GLA_TAU
    qh = heads(q, GLA_DK) * (GLA_DK ** -0.5)
    kh, vh, lah = heads(k, GLA_DK), heads(v, GLA_DV), heads(log_alpha, GLA_DK)
    bcum = jnp.cumsum(lah, axis=3)
    q_in = qh * jnp.exp(bcum)
    k_in = kh * jnp.exp(-bcum)
    k_out = kh * jnp.exp(bcum[..., -1:, :] - bcum)
    decay = jnp.exp(bcum[..., -1, :])
    causal = jnp.tril(jnp.ones((GLA_CHUNK, GLA_CHUNK), dtype=bool))
    a_intra = jnp.where(causal, jnp.einsum('nbhid,nbhjd->nbhij', q_in, k_in), 0.0)
    o_intra = jnp.einsum('nbhij,nbhjv->nbhiv', a_intra, vh)

    def step(state, xs):
        q_c, k_c, v_c, d_c = xs
        o_inter = jnp.einsum('bhid,bhdv->bhiv', q_c, state)
        new_state = d_c[..., None] * state + jnp.einsum('bhjd,bhjv->bhdv', k_c, v_c)
        return new_state, o_inter

    state0 = jnp.zeros((B, GLA_HEADS, GLA_DK, GLA_DV), f32)
    _, o_inter = lax.scan(step, state0, (q_in, k_out, vh, decay))
    o = (o_intra + o_inter).transpose(1, 0, 3, 2, 4).reshape(B, S, GLA_HEADS, GLA_DV)
    o = rms_norm(o, norm_g) * jax.nn.silu(r.astype(f32).reshape(B, S, GLA_HEADS, GLA_DV))
    return o.reshape(B, S, GLA_HEADS * GLA_DV).astype(q.dtype)


def hybrid_attention(h, w_in, w_out, q_norm_g, k_norm_g, cmp_pos, cmp_w1, cmp_w2,
                     gla_w_a2, gla_b_a, gla_norm_g):
    offsets = [int(o) for o in np.cumsum(EVEN_IN_SPLITS)[:-1]]
    (q, kc, vc, ks, vs, kw, vw, gl, gq, gk, gv, ga, gr) = jnp.split(h @ w_in, offsets, axis=-1)
    o_nsa = nsa_mixer(q, kc, vc, ks, vs, kw, vw, gl, q_norm_g, k_norm_g, cmp_pos, cmp_w1, cmp_w2)
    o_gla = gla_mixer(gq, gk, gv, ga, gr, gla_w_a2, gla_b_a, gla_norm_g)
    return jnp.concatenate([o_nsa, o_gla], axis=-1) @ w_out


def conformer_conv(h, w_pw1, b_pw1, w_dw, b_dw, ln_g, ln_b, w_pw2, b_pw2):
    u = h @ w_pw1 + b_pw1
    u = u[..., :D_MODEL] * jax.nn.sigmoid(u[..., D_MODEL:])
    u = lax.conv_general_dilated(u, w_dw[:, None, :], window_strides=(1,),
                                 padding=[(CONV_WIDTH - 1, 0)],
                                 dimension_numbers=('NWC', 'WIO', 'NWC'),
                                 feature_group_count=D_MODEL) + b_dw
    u = jax.nn.silu(layer_norm(u, ln_g, ln_b))
    return u @ w_pw2 + b_pw2


def swiglu(h, w_gate, w_up, w_down):
    return (jax.nn.silu(h @ w_gate) * (h @ w_up)) @ w_down


def setup_inputs(seed: int = 0) -> dict:
    key = jax.random.key(seed)
    keys = iter(jax.random.split(key, 28))
    D = D_MODEL

    def normal(shape, scale):
        return jax.random.normal(next(keys), shape, jnp.float32) * scale

    def gain(shape):
        return 1.0 + normal(shape, 0.02)

    return {
        'x': normal((BATCH, SEQ, D), 1.0),
        'c': normal((BATCH, D), 1.0),
        'w_mod': normal((D, N_MOD * D), 0.5 * D ** -0.5),
        'b_mod': normal((N_MOD * D,), 0.02),
        'ada_table': normal((DEPTH, N_MOD, D), 0.1),
        'norm_mix_g': gain((DEPTH, D)),
        'norm_ffn_g': gain((DEPTH, D)),
        'w_in': normal((N_EVEN, D, EVEN_IN), D ** -0.5),
        'w_out': normal((N_EVEN, MIX_OUT, D), MIX_OUT ** -0.5),
        'q_norm_g': gain((N_EVEN, HEAD_DIM)),
        'k_norm_g': gain((N_EVEN, 3, HEAD_DIM)),
        'cmp_pos': normal((N_EVEN, 2, CMP_BLOCK, HEAD_DIM), 0.5),
        'cmp_w1': normal((N_EVEN, 2, CMP_BLOCK * HEAD_DIM, CMP_HIDDEN), (CMP_BLOCK * HEAD_DIM) ** -0.5),
        'cmp_w2': normal((N_EVEN, 2, CMP_HIDDEN, HEAD_DIM), CMP_HIDDEN ** -0.5),
        'gla_w_a2': normal((N_EVEN, GLA_RANK, GLA_HEADS * GLA_DK), GLA_RANK ** -0.5),
        'gla_b_a': normal((N_EVEN, GLA_HEADS * GLA_DK), 0.1),
        'gla_norm_g': gain((N_EVEN, GLA_DV)),
        'cv_w_pw1': normal((N_ODD, D, 2 * D), D ** -0.5),
        'cv_b_pw1': normal((N_ODD, 2 * D), 0.02),
        'cv_w_dw': normal((N_ODD, CONV_WIDTH, D), CONV_WIDTH ** -0.5),
        'cv_b_dw': normal((N_ODD, D), 0.02),
        'cv_ln_g': gain((N_ODD, D)),
        'cv_ln_b': normal((N_ODD, D), 0.02),
        'cv_w_pw2': normal((N_ODD, D, D), D ** -0.5),
        'cv_b_pw2': normal((N_ODD, D), 0.02),
        'ffn_w_gate': normal((DEPTH, D, FFN_HIDDEN), D ** -0.5),
        'ffn_w_up': normal((DEPTH, D, FFN_HIDDEN), D ** -0.5),
        'ffn_w_down': normal((DEPTH, FFN_HIDDEN, D), FFN_HIDDEN ** -0.5),
    }


def reference(x, c, w_mod, b_mod, ada_table, norm_mix_g, norm_ffn_g, w_in, w_out, q_norm_g,
              k_norm_g, cmp_pos, cmp_w1, cmp_w2, gla_w_a2, gla_b_a, gla_norm_g, cv_w_pw1,
              cv_b_pw1, cv_w_dw, cv_b_dw, cv_ln_g, cv_ln_b, cv_w_pw2, cv_b_pw2, ffn_w_gate,
              ffn_w_up, ffn_w_down):
    B = x.shape[0]
    mod = (jax.nn.silu(c) @ w_mod + b_mod).reshape(B, N_MOD, D_MODEL)
    for layer in range(DEPTH):
        m = mod + ada_table[layer]
        sh_a, sc_a, g_a, sh_f, sc_f, g_f = [m[:, i, None, :] for i in range(N_MOD)]
        h = rms_norm(x, norm_mix_g[layer]) * (1.0 + sc_a) + sh_a
        j = layer // 2
        if layer % 2 == 0:
            y = hybrid_attention(h, w_in[j], w_out[j], q_norm_g[j], k_norm_g[j], cmp_pos[j],
                                 cmp_w1[j], cmp_w2[j], gla_w_a2[j], gla_b_a[j], gla_norm_g[j])
        else:
            y = conformer_conv(h, cv_w_pw1[j], cv_b_pw1[j], cv_w_dw[j], cv_b_dw[j], cv_ln_g[j],
                               cv_ln_b[j], cv_w_pw2[j], cv_b_pw2[j])
        x = x + g_a * y
        h = rms_norm(x, norm_ffn_g[layer]) * (1.0 + sc_f) + sh_f
        x = x + g_f * swiglu(h, ffn_w_gate[layer], ffn_w_up[layer], ffn_w_down[layer])
    return x
```

```python
import functools

import numpy as np
import jax
import jax.numpy as jnp
from jax import lax
from jax.experimental import pallas as pl
from jax.experimental.pallas import tpu as pltpu

F32 = jnp.float32
BF16 = jnp.bfloat16

HEAD_DIM = 128
NSA_HEADS = 16
NSA_KV_GROUPS = 4
NSA_HPG = NSA_HEADS // NSA_KV_GROUPS
CMP_BLOCK = 32
CMP_STRIDE = 16
SLC_BLOCK = 64
SLC_TOPK = 16
WINDOW = 512
ROPE_THETA = 500000.0
ROPE_DIM = HEAD_DIM // 4
GLA_HEADS = 4
GLA_DK = 256
GLA_DV = 512
GLA_RANK = 16
GLA_TAU = 16.0
GLA_CHUNK = 64
CONV_WIDTH = 31
N_MOD = 6
NSA_Q = NSA_HEADS * HEAD_DIM
NSA_KV = NSA_KV_GROUPS * HEAD_DIM

VMEM_LIMIT_BYTES = 56 * 1024 * 1024
LANES = 128
NEG_BIG = -1e30

ATTN_TQ = 128
CONV_TS = 64
CONV_HALO = 32
CONV_LANE_CHUNK = 512
GLA_TC = 256


def _cp(*sem):
    return pltpu.CompilerParams(dimension_semantics=sem, vmem_limit_bytes=VMEM_LIMIT_BYTES)


def _dot(a, b):
    return jnp.dot(a, b, preferred_element_type=F32)


def _dot_nt(a, b):
    return lax.dot_general(a, b, (((1,), (1,)), ((), ())), preferred_element_type=F32)


def _dot_tn(a, b):
    return lax.dot_general(a, b, (((0,), (0,)), ((), ())), preferred_element_type=F32)


def _silu(x):
    return x * jax.nn.sigmoid(x)


def _mm_kernel(x_ref, w_ref, o_ref):
    o_ref[...] = _dot(x_ref[...], w_ref[...]).astype(o_ref.dtype)


def _matmul(x, w, out_dtype, tm, tn):
    M, K = x.shape
    N = w.shape[1]
    return pl.pallas_call(
        _mm_kernel,
        out_shape=jax.ShapeDtypeStruct((M, N), out_dtype),
        grid=(M // tm, N // tn),
        in_specs=[pl.BlockSpec((tm, K), lambda i, j: (i, 0)),
                  pl.BlockSpec((K, tn), lambda i, j: (0, j))],
        out_specs=pl.BlockSpec((tm, tn), lambda i, j: (i, j)),
        compiler_params=_cp("parallel", "parallel"),
        name="matmul",
    )(x, w)


def _mm_glu_kernel(x_ref, wa_ref, wb_ref, ba_ref, bb_ref, o_ref, *, act_on_a):
    x = x_ref[...]
    a = _dot(x, wa_ref[...]) + ba_ref[...]
    b = _dot(x, wb_ref[...]) + bb_ref[...]
    y = _silu(a) * b if act_on_a else a * jax.nn.sigmoid(b)
    o_ref[...] = y.astype(o_ref.dtype)


def _matmul_glu(x, wa, wb, ba, bb, b_off, act_on_a, out_dtype, tm, tn):
    M, K = x.shape
    N = ba.shape[1] - b_off * tn if wa is wb else ba.shape[1]
    return pl.pallas_call(
        functools.partial(_mm_glu_kernel, act_on_a=act_on_a),
        out_shape=jax.ShapeDtypeStruct((M, N), out_dtype),
        grid=(M // tm, N // tn),
        in_specs=[pl.BlockSpec((tm, K), lambda i, j: (i, 0)),
                  pl.BlockSpec((K, tn), lambda i, j: (0, j)),
                  pl.BlockSpec((K, tn), lambda i, j: (0, j + b_off)),
                  pl.BlockSpec((1, tn), lambda i, j: (0, j)),
                  pl.BlockSpec((1, tn), lambda i, j: (0, j + b_off))],
        out_specs=pl.BlockSpec((tm, tn), lambda i, j: (i, j)),
        compiler_params=_cp("parallel", "parallel"),
        name="matmul_glu",
    )(x, wa, wb, ba, bb)


def _mm_res_kernel(x_ref, w_ref, b_ref, res_ref, g_ref, o_ref):
    y = _dot(x_ref[...], w_ref[...]) + b_ref[...]
    o_ref[...] = res_ref[...] + g_ref[0] * y


def _matmul_residual(x, w, bias, res, gate, rows_per_batch, tm, tn):
    M, K = x.shape
    N = w.shape[1]
    bpt = rows_per_batch // tm
    return pl.pallas_call(
        _mm_res_kernel,
        out_shape=jax.ShapeDtypeStruct((M, N), F32),
        grid=(M // tm, N // tn),
        in_specs=[pl.BlockSpec((tm, K), lambda i, j: (i, 0)),
                  pl.BlockSpec((K, tn), lambda i, j: (0, j)),
                  pl.BlockSpec((1, tn), lambda i, j: (0, j)),
                  pl.BlockSpec((tm, tn), lambda i, j: (i, j)),
                  pl.BlockSpec((1, 1, tn), lambda i, j: (i // bpt, 0, j))],
        out_specs=pl.BlockSpec((tm, tn), lambda i, j: (i, j)),
        compiler_params=_cp("parallel", "parallel"),
        name="matmul_residual",
    )(x, w, bias, res, gate)


def _mod_kernel(c_ref, w_ref, b_ref, o_ref):
    a = _silu(c_ref[...]).astype(BF16)
    o_ref[...] = _dot(a, w_ref[...].astype(BF16)) + b_ref[...]


def _ada_mod(c, w_mod, b_mod):
    B, D = c.shape
    N = w_mod.shape[1]
    tn = 512
    return pl.pallas_call(
        _mod_kernel,
        out_shape=jax.ShapeDtypeStruct((B, N), F32),
        grid=(N // tn,),
        in_specs=[pl.BlockSpec((B, D), lambda j: (0, 0)),
                  pl.BlockSpec((D, tn), lambda j: (0, j)),
                  pl.BlockSpec((1, tn), lambda j: (0, j))],
        out_specs=pl.BlockSpec((B, tn), lambda j: (0, j)),
        compiler_params=_cp("parallel"),
        name="ada_mod",
    )(c, w_mod, b_mod.reshape(1, N))


def _norm_mod_kernel(x_ref, g_ref, sc_ref, sh_ref, o_ref):
    x = x_ref[0]
    y = x * lax.rsqrt(jnp.mean(x * x, axis=-1, keepdims=True) + 1e-6)
    o_ref[0] = ((y * g_ref[...]) * (1.0 + sc_ref[0]) + sh_ref[0]).astype(o_ref.dtype)


def _norm_mod(x, g, sc, sh):
    B, S, D = x.shape
    ts = 256
    return pl.pallas_call(
        _norm_mod_kernel,
        out_shape=jax.ShapeDtypeStruct((B, S, D), BF16),
        grid=(B, S // ts),
        in_specs=[pl.BlockSpec((1, ts, D), lambda b, s: (b, s, 0)),
                  pl.BlockSpec((1, D), lambda b, s: (0, 0)),
                  pl.BlockSpec((1, 1, D), lambda b, s: (b, 0, 0)),
                  pl.BlockSpec((1, 1, D), lambda b, s: (b, 0, 0))],
        out_specs=pl.BlockSpec((1, ts, D), lambda b, s: (b, s, 0)),
        compiler_params=_cp("parallel", "parallel"),
        name="norm_mod",
    )(x, g.reshape(1, D), sc.reshape(B, 1, D), sh.reshape(B, 1, D))


def _rope_tables(pos):
    half = ROPE_DIM // 2
    inv_freq = ROPE_THETA ** (-jnp.arange(half, dtype=F32) / half)
    ang = pos.astype(F32)[:, None] * inv_freq[None, :]
    cos, sin = jnp.cos(ang), jnp.sin(ang)
    n = pos.shape[0]
    rest = HEAD_DIM - ROPE_DIM
    c = jnp.concatenate([cos, cos, jnp.ones((n, rest), F32)], axis=-1)
    s_lo = jnp.concatenate([-sin, jnp.zeros((n, HEAD_DIM - half), F32)], axis=-1)
    s_hi = jnp.concatenate([jnp.zeros((n, half), F32), sin, jnp.zeros((n, rest), F32)], axis=-1)
    return c, s_lo, s_hi


def _norm_rope(x, g, c, s_lo, s_hi):
    y = x * lax.rsqrt(jnp.mean(x * x, axis=-1, keepdims=True) + 1e-6) * g
    half = ROPE_DIM // 2
    return y * c + pltpu.roll(y, HEAD_DIM - half, 1) * s_lo + pltpu.roll(y, half, 1) * s_hi


def _nsa_prep_kernel(q_ref, ks_ref, vs_ref, kw_ref, vw_ref, c_ref, slo_ref, shi_ref, qg_ref, kg_ref,
                     qo_ref, kso_ref, vso_ref, kwo_ref, vwo_ref):
    c, s_lo, s_hi = c_ref[...], slo_ref[...], shi_ref[...]
    scale = HEAD_DIM ** -0.5
    for h in range(NSA_HEADS):
        sl = slice(h * HEAD_DIM, (h + 1) * HEAD_DIM)
        qo_ref[0, :, sl] = (_norm_rope(q_ref[0, :, sl], qg_ref[...], c, s_lo, s_hi) * scale).astype(BF16)
    for g in range(NSA_KV_GROUPS):
        sl = slice(g * HEAD_DIM, (g + 1) * HEAD_DIM)
        kso_ref[0, :, sl] = _norm_rope(ks_ref[0, :, sl], kg_ref[1:2, :], c, s_lo, s_hi).astype(BF16)
        kwo_ref[0, :, sl] = _norm_rope(kw_ref[0, :, sl], kg_ref[2:3, :], c, s_lo, s_hi).astype(BF16)
    vso_ref[0] = vs_ref[0].astype(BF16)
    vwo_ref[0] = vw_ref[0].astype(BF16)


def _nsa_prep(proj, tables, q_norm_g, k_norm_g):
    B, S, _ = proj.shape
    ts = 256
    kvb = NSA_Q // NSA_KV

    def kv_spec(n):
        return pl.BlockSpec((1, ts, NSA_KV), lambda b, s: (b, s, kvb + n))

    tab = pl.BlockSpec((ts, HEAD_DIM), lambda b, s: (s, 0))
    out_kv = pl.BlockSpec((1, ts, NSA_KV), lambda b, s: (b, s, 0))
    kv_shape = jax.ShapeDtypeStruct((B, S, NSA_KV), BF16)
    return pl.pallas_call(
        _nsa_prep_kernel,
        out_shape=(jax.ShapeDtypeStruct((B, S, NSA_Q), BF16), kv_shape, kv_shape, kv_shape, kv_shape),
        grid=(B, S // ts),
        in_specs=[pl.BlockSpec((1, ts, NSA_Q), lambda b, s: (b, s, 0)),
                  kv_spec(2), kv_spec(3), kv_spec(4), kv_spec(5), tab, tab, tab,
                  pl.BlockSpec((1, HEAD_DIM), lambda b, s: (0, 0)),
                  pl.BlockSpec((3, HEAD_DIM), lambda b, s: (0, 0))],
        out_specs=(pl.BlockSpec((1, ts, NSA_Q), lambda b, s: (b, s, 0)), out_kv, out_kv, out_kv, out_kv),
        compiler_params=_cp("parallel", "parallel"),
        name="nsa_prep",
    )(proj, proj, proj, proj, proj, *tables, q_norm_g.reshape(1, HEAD_DIM), k_norm_g)


def _compress_kernel(kc_ref, vc_ref, pos_ref, w1_ref, w2_ref, kg_ref, c_ref, slo_ref, shi_ref,
                     ko_ref, vo_ref):
    half_blk = CMP_BLOCK // 2
    n_seg = kc_ref.shape[1] // CMP_STRIDE

    def compress(tok_ref, j):
        u = jnp.zeros((n_seg, w1_ref.shape[2]), F32)
        v = jnp.zeros((n_seg, w1_ref.shape[2]), F32)
        for l in range(half_blk):
            x = tok_ref[0, pl.ds(l, n_seg, stride=CMP_STRIDE), :]
            xa = (x + pos_ref[j, l:l + 1, :]).astype(BF16)
            xb = (x + pos_ref[j, half_blk + l:half_blk + l + 1, :]).astype(BF16)
            u = u + _dot(xa, w1_ref[j, l * HEAD_DIM:(l + 1) * HEAD_DIM, :])
            v = v + _dot(xb, w1_ref[j, (half_blk + l) * HEAD_DIM:(half_blk + l + 1) * HEAD_DIM, :])
        h = u + pltpu.roll(v, n_seg - 1, 0)
        return _dot(jax.nn.gelu(h).astype(BF16), w2_ref[j])

    k = compress(kc_ref, 0)
    ko_ref[0, 0] = _norm_rope(k, kg_ref[0:1, :], c_ref[...], slo_ref[...], shi_ref[...]).astype(BF16)
    vo_ref[0, 0] = compress(vc_ref, 1).astype(BF16)


def _nsa_compress(proj, cmp_pos, cmp_w1, cmp_w2, k_norm_g, cmp_tables):
    B, S, _ = proj.shape
    G = NSA_KV_GROUPS
    n_seg = S // CMP_STRIDE
    kcb = NSA_Q // HEAD_DIM
    full2 = lambda b, g: (0, 0)
    full3 = lambda b, g: (0, 0, 0)
    out_spec = pl.BlockSpec((1, 1, n_seg, HEAD_DIM), lambda b, g: (b, g, 0, 0))
    out_shape = jax.ShapeDtypeStruct((B, G, n_seg, HEAD_DIM), BF16)
    return pl.pallas_call(
        _compress_kernel,
        out_shape=(out_shape, out_shape),
        grid=(B, G),
        in_specs=[pl.BlockSpec((1, S, HEAD_DIM), lambda b, g: (b, 0, kcb + g)),
                  pl.BlockSpec((1, S, HEAD_DIM), lambda b, g: (b, 0, kcb + G + g)),
                  pl.BlockSpec(cmp_pos.shape, full3),
                  pl.BlockSpec(cmp_w1.shape, full3),
                  pl.BlockSpec(cmp_w2.shape, full3),
                  pl.BlockSpec((3, HEAD_DIM), full2),
                  pl.BlockSpec((n_seg, HEAD_DIM), full2),
                  pl.BlockSpec((n_seg, HEAD_DIM), full2),
                  pl.BlockSpec((n_seg, HEAD_DIM), full2)],
        out_specs=(out_spec, out_spec),
        compiler_params=_cp("parallel", "parallel"),
        name="nsa_compress",
    )(proj, proj, cmp_pos, cmp_w1.astype(BF16), cmp_w2.astype(BF16), k_norm_g, *cmp_tables)


def _softmax_update(s, mask, m, l, acc, v):
    s = jnp.where(mask, s, NEG_BIG)
    m_new = jnp.maximum(m, jnp.max(s, axis=-1, keepdims=True))
    alpha = jnp.exp(m - m_new)
    p = jnp.where(mask, jnp.exp(s - m_new), 0.0)
    l_new = alpha * l + jnp.sum(p, axis=-1, keepdims=True)
    acc_new = alpha * acc + _dot(p.astype(BF16), v)
    return m_new, l_new, acc_new


def _nsa_attn_kernel(q_ref, kc_ref, vc_ref, ks_ref, vs_ref, kw_ref, vw_ref, gate_ref, ovl_ref, exp_ref,
                     o_ref, selm_ref):
    TQ = ATTN_TQ
    R = NSA_HPG * TQ
    qi = pl.program_id(2)
    t0 = qi * TQ
    q = jnp.concatenate([q_ref[0, :, h * HEAD_DIM:(h + 1) * HEAD_DIM] for h in range(NSA_HPG)], axis=0)
    trow = t0 + (lax.broadcasted_iota(jnp.int32, (R, LANES), 0) & (TQ - 1))
    col = lax.broadcasted_iota(jnp.int32, (R, LANES), 1)

    s = _dot_nt(q, kc_ref[0, 0])
    mask = (col * CMP_STRIDE + (CMP_BLOCK - 1)) <= trow
    s = jnp.where(mask, s, NEG_BIG)
    p = jnp.where(mask, jnp.exp(s - jnp.max(s, axis=-1, keepdims=True)), 0.0)
    l = jnp.sum(p, axis=-1, keepdims=True)
    p = p / jnp.where(l > 0.0, l, 1.0)
    pb = p.astype(BF16)
    o_cmp = _dot(pb, vc_ref[0, 0])

    p_heads = jnp.concatenate([pb[h * TQ:(h + 1) * TQ] for h in range(NSA_HPG)], axis=1)
    imp = _dot(p_heads, ovl_ref[...])
    t = t0 + lax.broadcasted_iota(jnp.int32, (TQ, LANES), 0)
    blk = lax.broadcasted_iota(jnp.int32, (TQ, LANES), 1)
    cur = t // SLC_BLOCK
    forced = (blk == 0) | (blk == cur) | (blk == cur - 1)
    valid = blk * SLC_BLOCK <= t
    val = jnp.where(forced, jnp.inf, jnp.where(valid, imp, -jnp.inf))
    n_slc = exp_ref.shape[1] // SLC_BLOCK
    rank = jnp.zeros((TQ, LANES), F32)
    for i in range(n_slc):
        vi = val[:, i:i + 1]
        ahead = (vi > val) | ((vi == val) & (blk > i))
        rank = rank + jnp.where(ahead, 1.0, 0.0)
    sel = jnp.where((rank < float(SLC_TOPK)) & (val > -jnp.inf), 1.0, 0.0).astype(BF16)
    for kt in range(selm_ref.shape[0]):
        selm_ref[kt] = _dot(sel, exp_ref[:, kt * LANES:(kt + 1) * LANES])

    m0 = jnp.full((R, 1), NEG_BIG, F32)
    l0 = jnp.zeros((R, 1), F32)
    a0 = jnp.zeros((R, HEAD_DIM), F32)

    def slc_body(kt, carry):
        off = pl.multiple_of(kt * LANES, LANES)
        sm = selm_ref[kt]
        selmask = jnp.concatenate([sm] * NSA_HPG, axis=0) > 0.5
        kpos = off + col
        s = _dot_nt(q, ks_ref[0, pl.ds(off, LANES), :])
        return _softmax_update(s, selmask & (kpos <= trow), *carry, vs_ref[0, pl.ds(off, LANES), :])

    _, l_s, a_s = lax.fori_loop(0, qi + 1, slc_body, (m0, l0, a0))
    o_slc = a_s / jnp.where(l_s > 0.0, l_s, 1.0)

    def win_body(kt, carry):
        off = pl.multiple_of(kt * LANES, LANES)
        kpos = off + col
        s = _dot_nt(q, kw_ref[0, pl.ds(off, LANES), :])
        return _softmax_update(s, (kpos <= trow) & (trow - kpos < WINDOW), *carry,
                               vw_ref[0, pl.ds(off, LANES), :])

    _, l_w, a_w = lax.fori_loop(jnp.maximum(qi - WINDOW // LANES, 0), qi + 1, win_body, (m0, l0, a0))
    o_win = a_w / jnp.where(l_w > 0.0, l_w, 1.0)

    gate = jax.nn.sigmoid(gate_ref[0, 0])
    for h in range(NSA_HPG):
        rows = slice(h * TQ, (h + 1) * TQ)
        o = (gate[:, 3 * h:3 * h + 1] * o_cmp[rows] + gate[:, 3 * h + 1:3 * h + 2] * o_slc[rows]
             + gate[:, 3 * h + 2:3 * h + 3] * o_win[rows])
        o_ref[0, :, h * HEAD_DIM:(h + 1) * HEAD_DIM] = o.astype(o_ref.dtype)


def _nsa_attention(qn, kcmp, vcmp, ksn, vsb, kwn, vwb, gate_logits):
    B, S, _ = qn.shape
    G = NSA_KV_GROUPS
    TQ = ATTN_TQ
    n_cmp = S // CMP_STRIDE
    n_slc = S // SLC_BLOCK
    assert n_cmp == LANES and n_slc <= LANES and WINDOW % LANES == 0 and TQ == LANES
    cmp_start = np.arange(n_cmp) * CMP_STRIDE
    slc_start = np.arange(LANES) * SLC_BLOCK
    overlap = np.clip(np.minimum(cmp_start[:, None] + CMP_BLOCK, slc_start[None, :] + SLC_BLOCK)
                      - np.maximum(cmp_start[:, None], slc_start[None, :]), 0, None) / CMP_STRIDE
    overlap[:, n_slc:] = 0.0
    ovl = jnp.asarray(np.tile(overlap, (NSA_HPG, 1)), dtype=BF16)
    expand = jnp.asarray(np.arange(LANES)[:, None] == (np.arange(S)[None, :] // SLC_BLOCK), dtype=BF16)
    gq = NSA_HPG * HEAD_DIM
    q_spec = pl.BlockSpec((1, TQ, gq), lambda b, g, i: (b, i, g))
    cmp_spec = pl.BlockSpec((1, 1, n_cmp, HEAD_DIM), lambda b, g, i: (b, g, 0, 0))
    kv_spec = pl.BlockSpec((1, S, HEAD_DIM), lambda b, g, i: (b, 0, g))
    return pl.pallas_call(
        _nsa_attn_kernel,
        out_shape=jax.ShapeDtypeStruct((B, S, NSA_Q), BF16),
        grid=(B, G, S // TQ),
        in_specs=[q_spec, cmp_spec, cmp_spec, kv_spec, kv_spec, kv_spec, kv_spec,
                  pl.BlockSpec((1, 1, TQ, 3 * NSA_HPG), lambda b, g, i: (b, g, i, 0)),
                  pl.BlockSpec(ovl.shape, lambda b, g, i: (0, 0)),
                  pl.BlockSpec(expand.shape, lambda b, g, i: (0, 0))],
        out_specs=q_spec,
        scratch_shapes=[pltpu.VMEM((S // LANES, TQ, LANES), F32)],
        compiler_params=_cp("parallel", "parallel", "parallel"),
        name="nsa_attention",
    )(qn, kcmp, vcmp, ksn, vsb, kwn, vwb, gate_logits, ovl, expand)


def _gla_kernel(q_ref, k_ref, v_ref, r_ref, a_ref, wa_ref, ba_ref, ng_ref, o_ref, state_ref, *, a_off):
    C = GLA_CHUNK

    @pl.when(pl.program_id(2) == 0)
    def _():
        state_ref[...] = jnp.zeros_like(state_ref)

    row = lax.broadcasted_iota(jnp.int32, (C, C), 0)
    colc = lax.broadcasted_iota(jnp.int32, (C, C), 1)
    causal = colc <= row
    tri = jnp.where(causal, 1.0, 0.0).astype(BF16)
    wa = wa_ref[...].astype(BF16)
    for c in range(q_ref.shape[1] // C):
        rows = slice(c * C, (c + 1) * C)
        a_low = a_ref[0, rows, a_off:a_off + GLA_RANK].astype(BF16)
        z = _dot(a_low, wa) + ba_ref[...]
        la = (jnp.minimum(z, 0.0) - jnp.log1p(jnp.exp(-jnp.abs(z)))) / GLA_TAU
        hi = la.astype(BF16)
        r1 = la - hi.astype(F32)
        mid = r1.astype(BF16)
        lo = (r1 - mid.astype(F32)).astype(BF16)
        bcum = _dot(tri, hi) + _dot(tri, mid) + _dot(tri, lo)
        blast = bcum[C - 1:C, :]
        kh = k_ref[0, rows, :]
        q_in = (q_ref[0, rows, :] * (GLA_DK ** -0.5) * jnp.exp(bcum)).astype(BF16)
        k_in = (kh * jnp.exp(-bcum)).astype(BF16)
        k_out = (kh * jnp.exp(blast - bcum)).astype(BF16)
        vb = v_ref[0, rows, :].astype(BF16)
        a_intra = jnp.where(causal, _dot_nt(q_in, k_in), 0.0).astype(BF16)
        state = state_ref[...]
        o = _dot(a_intra, vb) + _dot_nt(q_in, state.astype(BF16))
        state_ref[...] = state * jnp.exp(blast) + _dot_tn(vb, k_out)
        y = o * lax.rsqrt(jnp.mean(o * o, axis=-1, keepdims=True) + 1e-6) * ng_ref[...]
        o_ref[0, rows, :] = (y * _silu(r_ref[0, rows, :])).astype(o_ref.dtype)


def _gla(proj, proj_small, a_off, w_a2, b_a, norm_g, q_col):
    B, S, _ = proj.shape
    H, DK, DV, TC = GLA_HEADS, GLA_DK, GLA_DV, GLA_TC
    qb = q_col // DK
    kb = qb + H
    vb = (q_col + 2 * H * DK) // DV
    rb = vb + H
    return pl.pallas_call(
        functools.partial(_gla_kernel, a_off=a_off),
        out_shape=jax.ShapeDtypeStruct((B, S, H * DV), BF16),
        grid=(B, H, S // TC),
        in_specs=[pl.BlockSpec((1, TC, DK), lambda b, h, c: (b, c, qb + h)),
                  pl.BlockSpec((1, TC, DK), lambda b, h, c: (b, c, kb + h)),
                  pl.BlockSpec((1, TC, DV), lambda b, h, c: (b, c, vb + h)),
                  pl.BlockSpec((1, TC, DV), lambda b, h, c: (b, c, rb + h)),
                  pl.BlockSpec((1, TC, LANES), lambda b, h, c: (b, c, 0)),
                  pl.BlockSpec((GLA_RANK, DK), lambda b, h, c: (0, h)),
                  pl.BlockSpec((1, DK), lambda b, h, c: (0, h)),
                  pl.BlockSpec((1, DV), lambda b, h, c: (0, 0))],
        out_specs=pl.BlockSpec((1, TC, DV), lambda b, h, c: (b, c, h)),
        scratch_shapes=[pltpu.VMEM((DV, DK), F32)],
        compiler_params=_cp("parallel", "parallel", "arbitrary"),
        name="gla",
    )(proj, proj, proj, proj, proj_small, w_a2, b_a.reshape(1, H * DK), norm_g.reshape(1, DV))


def _conv_ln_kernel(u_ref, halo_ref, w_ref, b_ref, g_ref, beta_ref, o_ref, cat_ref, y_ref):
    TS, HALO, CH = CONV_TS, CONV_HALO, CONV_LANE_CHUNK
    D = u_ref.shape[2]

    @pl.when(pl.program_id(1) == 0)
    def _():
        cat_ref[0:HALO, :] = jnp.zeros((HALO, D), F32)

    @pl.when(pl.program_id(1) > 0)
    def _():
        cat_ref[0:HALO, :] = halo_ref[0]

    cat_ref[HALO:HALO + TS, :] = u_ref[0]
    first = HALO - (CONV_WIDTH - 1)
    total = jnp.zeros((TS, 1), F32)
    for c in range(D // CH):
        lanes = slice(c * CH, (c + 1) * CH)
        acc = jnp.zeros((TS, CH), F32) + b_ref[:, lanes]
        for k in range(CONV_WIDTH):
            acc = acc + cat_ref[first + k:first + k + TS, lanes] * w_ref[k:k + 1, lanes]
        y_ref[:, lanes] = acc
        total = total + jnp.sum(acc, axis=-1, keepdims=True)
    mu = total / D
    sq = jnp.zeros((TS, 1), F32)
    for c in range(D // CH):
        d = y_ref[:, c * CH:(c + 1) * CH] - mu
        sq = sq + jnp.sum(d * d, axis=-1, keepdims=True)
    inv = lax.rsqrt(sq / D + 1e-5)
    for c in range(D // CH):
        lanes = slice(c * CH, (c + 1) * CH)
        z = (y_ref[:, lanes] - mu) * inv * g_ref[:, lanes] + beta_ref[:, lanes]
        o_ref[0, :, lanes] = _silu(z).astype(o_ref.dtype)


def _conv_ln_silu(u, w_dw, b_dw, ln_g, ln_b):
    B, S, D = u.shape
    TS, HALO = CONV_TS, CONV_HALO
    ratio = TS // HALO
    vec = pl.BlockSpec((1, D), lambda b, s: (0, 0))
    return pl.pallas_call(
        _conv_ln_kernel,
        out_shape=jax.ShapeDtypeStruct((B, S, D), BF16),
        grid=(B, S // TS),
        in_specs=[pl.BlockSpec((1, TS, D), lambda b, s: (b, s, 0)),
                  pl.BlockSpec((1, HALO, D), lambda b, s: (b, jnp.maximum(s * ratio - 1, 0), 0)),
                  pl.BlockSpec((CONV_WIDTH, D), lambda b, s: (0, 0)),
                  vec, vec, vec],
        out_specs=pl.BlockSpec((1, TS, D), lambda b, s: (b, s, 0)),
        scratch_shapes=[pltpu.VMEM((HALO + TS, D), F32), pltpu.VMEM((TS, D), F32)],
        compiler_params=_cp("parallel", "parallel"),
        name="conv_ln_silu",
    )(u, u, w_dw, b_dw.reshape(1, D), ln_g.reshape(1, D), ln_b.reshape(1, D))


def _hybrid_attention(h, x2, gate, w_in, w_out, q_norm_g, k_norm_g, cmp_pos, cmp_w1, cmp_w2,
                      gla_w_a2, gla_b_a, gla_norm_g, B, S):
    D = h.shape[1]
    n_gate = NSA_HEADS * 3
    o_gl = NSA_Q + 6 * NSA_KV
    o_gq = o_gl + n_gate
    o_ga = o_gq + 2 * GLA_HEADS * GLA_DK + GLA_HEADS * GLA_DV
    o_gr = o_ga + GLA_RANK
    w_main = jnp.concatenate([w_in[:, :o_gl], w_in[:, o_gq:o_ga], w_in[:, o_gr:]], axis=1).astype(BF16)
    w_small = jnp.concatenate([w_in[:, o_gl:o_gq], w_in[:, o_ga:o_gr],
                               jnp.zeros((D, LANES - n_gate - GLA_RANK), w_in.dtype)], axis=1).astype(BF16)
    proj = _matmul(h, w_main, F32, 1024, 512).reshape(B, S, -1)
    proj_small = _matmul(h, w_small, F32, 1024, LANES).reshape(B, S, LANES)

    t = jnp.arange(S, dtype=jnp.int32)
    cmp_end = jnp.arange(S // CMP_STRIDE, dtype=jnp.int32) * CMP_STRIDE + (CMP_BLOCK - 1)
    qn, ksn, vsb, kwn, vwb = _nsa_prep(proj, _rope_tables(t), q_norm_g, k_norm_g)
    kcmp, vcmp = _nsa_compress(proj, cmp_pos, cmp_w1, cmp_w2, k_norm_g, _rope_tables(cmp_end))
    gate_logits = proj_small[:, :, :n_gate].reshape(B, S, NSA_KV_GROUPS, 3 * NSA_HPG).transpose(0, 2, 1, 3)
    o_nsa = _nsa_attention(qn, kcmp, vcmp, ksn, vsb, kwn, vwb, gate_logits)
    o_gla = _gla(proj, proj_small, n_gate, gla_w_a2, gla_b_a, gla_norm_g, o_gl)
    o = jnp.concatenate([o_nsa, o_gla], axis=-1).reshape(B * S, -1)
    return _matmul_residual(o, w_out.astype(BF16), jnp.zeros((1, D), F32), x2, gate, S, 1024, 512)


def _conformer(h, x2, gate, w_pw1, b_pw1, w_dw, b_dw, ln_g, ln_b, w_pw2, b_pw2, B, S):
    D = h.shape[1]
    tn = 256
    w1 = w_pw1.astype(BF16)
    b1 = b_pw1.reshape(1, 2 * D)
    u = _matmul_glu(h, w1, w1, b1, b1, D // tn, False, F32, 1024, tn)
    v = _conv_ln_silu(u.reshape(B, S, D), w_dw, b_dw, ln_g, ln_b).reshape(B * S, D)
    return _matmul_residual(v, w_pw2.astype(BF16), b_pw2.reshape(1, D), x2, gate, S, 1024, 512)


def _swiglu(h, x2, gate, w_gate, w_up, w_down, S):
    D = h.shape[1]
    F = w_gate.shape[1]
    zeros = jnp.zeros((1, F), F32)
    act = _matmul_glu(h, w_gate.astype(BF16), w_up.astype(BF16), zeros, zeros, 0, True, BF16, 1024, 256)
    return _matmul_residual(act, w_down.astype(BF16), jnp.zeros((1, D), F32), x2, gate, S, 512, 256)


def kernel(x, c, w_mod, b_mod, ada_table, norm_mix_g, norm_ffn_g, w_in, w_out, q_norm_g, k_norm_g, cmp_pos, cmp_w1, cmp_w2, gla_w_a2, gla_b_a, gla_norm_g, cv_w_pw1, cv_b_pw1, cv_w_dw, cv_b_dw, cv_ln_g, cv_ln_b, cv_w_pw2, cv_b_pw2, ffn_w_gate, ffn_w_up, ffn_w_down):
    B, S, D = x.shape
    depth = ada_table.shape[0]
    mod = _ada_mod(c, w_mod, b_mod).reshape(B, N_MOD, D)
    for layer in range(depth):
        m = mod + ada_table[layer]
        sh_a, sc_a, g_a, sh_f, sc_f, g_f = [m[:, i, :] for i in range(N_MOD)]
        h = _norm_mod(x, norm_mix_g[layer], sc_a, sh_a).reshape(B * S, D)
        x2 = x.reshape(B * S, D)
        g_a3 = g_a.reshape(B, 1, D)
        j = layer // 2
        if layer % 2 == 0:
            x2 = _hybrid_attention(h, x2, g_a3, w_in[j], w_out[j], q_norm_g[j], k_norm_g[j], cmp_pos[j],
                                   cmp_w1[j], cmp_w2[j], gla_w_a2[j], gla_b_a[j], gla_norm_g[j], B, S)
        else:
            x2 = _conformer(h, x2, g_a3, cv_w_pw1[j], cv_b_pw1[j], cv_w_dw[j], cv_b_dw[j], cv_ln_g[j],
                            cv_ln_b[j], cv_w_pw2[j], cv_b_pw2[j], B, S)
        x = x2.reshape(B, S, D)
        h = _norm_mod(x, norm_ffn_g[layer], sc_f, sh_f).reshape(B * S, D)
        x2 = _swiglu(h, x2, g_f.reshape(B, 1, D), ffn_w_gate[layer], ffn_w_up[layer], ffn_w_down[layer], S)
        x = x2.reshape(B, S, D)
    return x
```

```python
import functools

import numpy as np
import jax
import jax.numpy as jnp
from jax import lax
from jax.experimental import pallas as pl
from jax.experimental.pallas import tpu as pltpu

F32 = jnp.float32
BF16 = jnp.bfloat16

HEAD_DIM = 128
NSA_HEADS = 16
NSA_KV_GROUPS = 4
NSA_HPG = NSA_HEADS // NSA_KV_GROUPS
CMP_BLOCK = 32
CMP_STRIDE = 16
SLC_BLOCK = 64
SLC_TOPK = 16
WINDOW = 512
ROPE_THETA = 500000.0
ROPE_DIM = HEAD_DIM // 4
GLA_HEADS = 4
GLA_DK = 256
GLA_DV = 512
GLA_RANK = 16
GLA_TAU = 16.0
GLA_CHUNK = 64
CONV_WIDTH = 31
N_MOD = 6
NSA_Q = NSA_HEADS * HEAD_DIM
NSA_KV = NSA_KV_GROUPS * HEAD_DIM

VMEM_LIMIT_BYTES = 56 * 1024 * 1024
LANES = 128
SUBLANES = 8
NEG_BIG = -1e30
M_INIT = -1e29

ATTN_TQ = 128
CONV_TS = 128
CONV_HALO = 32
CONV_LANE_CHUNK = 128
GLA_TC = 256


def _cp(*sem):
    return pltpu.CompilerParams(dimension_semantics=sem, vmem_limit_bytes=VMEM_LIMIT_BYTES)


def _dot(a, b):
    return jnp.dot(a, b, preferred_element_type=F32)


def _dot_nt(a, b):
    return lax.dot_general(a, b, (((1,), (1,)), ((), ())), preferred_element_type=F32)


def _dot_tn(a, b):
    return lax.dot_general(a, b, (((0,), (0,)), ((), ())), preferred_element_type=F32)


def _silu(x):
    return x * jax.nn.sigmoid(x)


def _cast_weights_once(w_refs, wbf_refs):
    @pl.when(pl.program_id(1) == 0)
    def _():
        for w_ref, wbf_ref in zip(w_refs, wbf_refs):
            wbf_ref[...] = w_ref[...].astype(BF16)


def _mm_kernel(x_ref, w_ref, o_ref, wbf_ref):
    _cast_weights_once((w_ref,), (wbf_ref,))
    o_ref[...] = _dot(x_ref[...], wbf_ref[...]).astype(o_ref.dtype)


def _matmul(x, w, out_dtype, tm, tn):
    M, K = x.shape
    N = w.shape[1]
    return pl.pallas_call(
        _mm_kernel,
        out_shape=jax.ShapeDtypeStruct((M, N), out_dtype),
        grid=(N // tn, M // tm),
        in_specs=[pl.BlockSpec((tm, K), lambda j, i: (i, 0)),
                  pl.BlockSpec((K, tn), lambda j, i: (0, j))],
        out_specs=pl.BlockSpec((tm, tn), lambda j, i: (i, j)),
        scratch_shapes=[pltpu.VMEM((K, tn), BF16)],
        compiler_params=_cp("arbitrary", "arbitrary"),
        name="matmul",
    )(x, w)


def _mm_glu_kernel(x_ref, wa_ref, wb_ref, ba_ref, bb_ref, o_ref, wabf_ref, wbbf_ref, *, act_on_a):
    _cast_weights_once((wa_ref, wb_ref), (wabf_ref, wbbf_ref))
    x = x_ref[...]
    a = _dot(x, wabf_ref[...]) + ba_ref[...]
    b = _dot(x, wbbf_ref[...]) + bb_ref[...]
    y = _silu(a) * b if act_on_a else a * jax.nn.sigmoid(b)
    o_ref[...] = y.astype(o_ref.dtype)


def _matmul_glu(x, wa, wb, ba, bb, b_off, act_on_a, out_dtype, tm, tn):
    M, K = x.shape
    N = ba.shape[1] - b_off * tn if wa is wb else ba.shape[1]
    return pl.pallas_call(
        functools.partial(_mm_glu_kernel, act_on_a=act_on_a),
        out_shape=jax.ShapeDtypeStruct((M, N), out_dtype),
        grid=(N // tn, M // tm),
        in_specs=[pl.BlockSpec((tm, K), lambda j, i: (i, 0)),
                  pl.BlockSpec((K, tn), lambda j, i: (0, j)),
                  pl.BlockSpec((K, tn), lambda j, i: (0, j + b_off)),
                  pl.BlockSpec((1, tn), lambda j, i: (0, j)),
                  pl.BlockSpec((1, tn), lambda j, i: (0, j + b_off))],
        out_specs=pl.BlockSpec((tm, tn), lambda j, i: (i, j)),
        scratch_shapes=[pltpu.VMEM((K, tn), BF16), pltpu.VMEM((K, tn), BF16)],
        compiler_params=_cp("arbitrary", "arbitrary"),
        name="matmul_glu",
    )(x, wa, wb, ba, bb)


def _mm_res_kernel(*refs, n_x):
    x_refs = refs[:n_x]
    w_ref, b_ref, res_ref, g_ref, o_ref, wbf_ref = refs[n_x:]
    _cast_weights_once((w_ref,), (wbf_ref,))
    y = b_ref[...]
    off = 0
    for x_ref in x_refs:
        k = x_ref.shape[1]
        y = y + _dot(x_ref[...], wbf_ref[off:off + k, :])
        off += k
    o_ref[...] = res_ref[...] + g_ref[0] * y


def _matmul_residual(xs, w, bias, res, gate, rows_per_batch, tm, tn):
    M = xs[0].shape[0]
    K, N = w.shape
    bpt = rows_per_batch // tm
    return pl.pallas_call(
        functools.partial(_mm_res_kernel, n_x=len(xs)),
        out_shape=jax.ShapeDtypeStruct((M, N), F32),
        grid=(N // tn, M // tm),
        in_specs=[pl.BlockSpec((tm, x.shape[1]), lambda j, i: (i, 0)) for x in xs] + [
                  pl.BlockSpec((K, tn), lambda j, i: (0, j), pipeline_mode=pl.Buffered(1)),
                  pl.BlockSpec((1, tn), lambda j, i: (0, j)),
                  pl.BlockSpec((tm, tn), lambda j, i: (i, j)),
                  pl.BlockSpec((1, 1, tn), lambda j, i: (i // bpt, 0, j))],
        out_specs=pl.BlockSpec((tm, tn), lambda j, i: (i, j)),
        scratch_shapes=[pltpu.VMEM((K, tn), BF16)],
        compiler_params=_cp("arbitrary", "arbitrary"),
        name="matmul_residual",
    )(*xs, w, bias, res, gate)


def _mod_kernel(c_ref, w_ref, b_ref, o_ref):
    a = _silu(c_ref[...]).astype(BF16)
    o_ref[...] = _dot(a, w_ref[...].astype(BF16)) + b_ref[...]


def _ada_mod(c, w_mod, b_mod):
    B, D = c.shape
    N = w_mod.shape[1]
    tn = 512
    return pl.pallas_call(
        _mod_kernel,
        out_shape=jax.ShapeDtypeStruct((B, N), F32),
        grid=(N // tn,),
        in_specs=[pl.BlockSpec((B, D), lambda j: (0, 0)),
                  pl.BlockSpec((D, tn), lambda j: (0, j)),
                  pl.BlockSpec((1, tn), lambda j: (0, j))],
        out_specs=pl.BlockSpec((B, tn), lambda j: (0, j)),
        compiler_params=_cp("parallel"),
        name="ada_mod",
    )(c, w_mod, b_mod.reshape(1, N))


def _norm_mod_kernel(x_ref, g_ref, sc_ref, sh_ref, o_ref):
    x = x_ref[0]
    y = x * lax.rsqrt(jnp.mean(x * x, axis=-1, keepdims=True) + 1e-6)
    o_ref[0] = ((y * g_ref[...]) * (1.0 + sc_ref[0]) + sh_ref[0]).astype(o_ref.dtype)


def _norm_mod(x, g, sc, sh):
    B, S, D = x.shape
    ts = 256
    return pl.pallas_call(
        _norm_mod_kernel,
        out_shape=jax.ShapeDtypeStruct((B, S, D), BF16),
        grid=(B, S // ts),
        in_specs=[pl.BlockSpec((1, ts, D), lambda b, s: (b, s, 0)),
                  pl.BlockSpec((1, D), lambda b, s: (0, 0)),
                  pl.BlockSpec((1, 1, D), lambda b, s: (b, 0, 0)),
                  pl.BlockSpec((1, 1, D), lambda b, s: (b, 0, 0))],
        out_specs=pl.BlockSpec((1, ts, D), lambda b, s: (b, s, 0)),
        compiler_params=_cp("parallel", "parallel"),
        name="norm_mod",
    )(x, g.reshape(1, D), sc.reshape(B, 1, D), sh.reshape(B, 1, D))


def _rope_tables(pos):
    half = ROPE_DIM // 2
    inv_freq = ROPE_THETA ** (-jnp.arange(half, dtype=F32) / half)
    ang = pos.astype(F32)[:, None] * inv_freq[None, :]
    cos, sin = jnp.cos(ang), jnp.sin(ang)
    n = pos.shape[0]
    rest = HEAD_DIM - ROPE_DIM
    c = jnp.concatenate([cos, cos, jnp.ones((n, rest), F32)], axis=-1)
    s_lo = jnp.concatenate([-sin, jnp.zeros((n, HEAD_DIM - half), F32)], axis=-1)
    s_hi = jnp.concatenate([jnp.zeros((n, half), F32), sin, jnp.zeros((n, rest), F32)], axis=-1)
    return c, s_lo, s_hi


def _norm_rope(x, g, c, s_lo, s_hi):
    y = x * lax.rsqrt(jnp.mean(x * x, axis=-1, keepdims=True) + 1e-6) * g
    half = ROPE_DIM // 2
    return y * c + pltpu.roll(y, HEAD_DIM - half, 1) * s_lo + pltpu.roll(y, half, 1) * s_hi


def _nsa_prep_kernel(q_ref, ks_ref, vs_ref, kw_ref, vw_ref, c_ref, slo_ref, shi_ref, qg_ref, kg_ref,
                     qo_ref, kso_ref, vso_ref, kwo_ref, vwo_ref):
    c, s_lo, s_hi = c_ref[...], slo_ref[...], shi_ref[...]
    scale = HEAD_DIM ** -0.5
    for h in range(NSA_HEADS):
        sl = slice(h * HEAD_DIM, (h + 1) * HEAD_DIM)
        qo_ref[0, :, sl] = (_norm_rope(q_ref[0, :, sl], qg_ref[...], c, s_lo, s_hi) * scale).astype(BF16)
    for g in range(NSA_KV_GROUPS):
        sl = slice(g * HEAD_DIM, (g + 1) * HEAD_DIM)
        kso_ref[0, :, sl] = _norm_rope(ks_ref[0, :, sl], kg_ref[1:2, :], c, s_lo, s_hi).astype(BF16)
        kwo_ref[0, :, sl] = _norm_rope(kw_ref[0, :, sl], kg_ref[2:3, :], c, s_lo, s_hi).astype(BF16)
    vso_ref[0] = vs_ref[0].T.astype(BF16)
    vwo_ref[0] = vw_ref[0].T.astype(BF16)


def _nsa_prep(proj, tables, q_norm_g, k_norm_g):
    B, S, _ = proj.shape
    ts = 256
    kvb = NSA_Q // NSA_KV

    def kv_spec(n):
        return pl.BlockSpec((1, ts, NSA_KV), lambda b, s: (b, s, kvb + n))

    tab = pl.BlockSpec((ts, HEAD_DIM), lambda b, s: (s, 0))
    out_k = pl.BlockSpec((1, ts, NSA_KV), lambda b, s: (b, s, 0))
    out_vt = pl.BlockSpec((1, NSA_KV, ts), lambda b, s: (b, 0, s))
    k_shape = jax.ShapeDtypeStruct((B, S, NSA_KV), BF16)
    vt_shape = jax.ShapeDtypeStruct((B, NSA_KV, S), BF16)
    return pl.pallas_call(
        _nsa_prep_kernel,
        out_shape=(jax.ShapeDtypeStruct((B, S, NSA_Q), BF16), k_shape, vt_shape, k_shape, vt_shape),
        grid=(B, S // ts),
        in_specs=[pl.BlockSpec((1, ts, NSA_Q), lambda b, s: (b, s, 0)),
                  kv_spec(2), kv_spec(3), kv_spec(4), kv_spec(5), tab, tab, tab,
                  pl.BlockSpec((1, HEAD_DIM), lambda b, s: (0, 0)),
                  pl.BlockSpec((3, HEAD_DIM), lambda b, s: (0, 0))],
        out_specs=(pl.BlockSpec((1, ts, NSA_Q), lambda b, s: (b, s, 0)), out_k, out_vt, out_k, out_vt),
        compiler_params=_cp("parallel", "parallel"),
        name="nsa_prep",
    )(proj, proj, proj, proj, proj, *tables, q_norm_g.reshape(1, HEAD_DIM), k_norm_g)


def _compress_kernel(kc_ref, vc_ref, pos_ref, w1_ref, w2_ref, kg_ref, c_ref, slo_ref, shi_ref,
                     ko_ref, vo_ref):
    half_blk = CMP_BLOCK // 2
    n_seg = kc_ref.shape[1] // CMP_STRIDE

    def compress(tok_ref, j):
        u = jnp.zeros((n_seg, w1_ref.shape[2]), F32)
        v = jnp.zeros((n_seg, w1_ref.shape[2]), F32)
        for l in range(half_blk):
            x = tok_ref[0, pl.ds(l, n_seg, stride=CMP_STRIDE), :]
            xa = (x + pos_ref[j, l:l + 1, :]).astype(BF16)
            xb = (x + pos_ref[j, half_blk + l:half_blk + l + 1, :]).astype(BF16)
            u = u + _dot(xa, w1_ref[j, l * HEAD_DIM:(l + 1) * HEAD_DIM, :])
            v = v + _dot(xb, w1_ref[j, (half_blk + l) * HEAD_DIM:(half_blk + l + 1) * HEAD_DIM, :])
        h = u + pltpu.roll(v, n_seg - 1, 0)
        return _dot(jax.nn.gelu(h).astype(BF16), w2_ref[j])

    k = compress(kc_ref, 0)
    ko_ref[0, 0] = _norm_rope(k, kg_ref[0:1, :], c_ref[...], slo_ref[...], shi_ref[...]).astype(BF16)
    vo_ref[0, 0] = compress(vc_ref, 1).T.astype(BF16)


def _nsa_compress(proj, cmp_pos, cmp_w1, cmp_w2, k_norm_g, cmp_tables):
    B, S, _ = proj.shape
    G = NSA_KV_GROUPS
    n_seg = S // CMP_STRIDE
    kcb = NSA_Q // HEAD_DIM
    full2 = lambda b, g: (0, 0)
    full3 = lambda b, g: (0, 0, 0)
    return pl.pallas_call(
        _compress_kernel,
        out_shape=(jax.ShapeDtypeStruct((B, G, n_seg, HEAD_DIM), BF16),
                   jax.ShapeDtypeStruct((B, G, HEAD_DIM, n_seg), BF16)),
        grid=(B, G),
        in_specs=[pl.BlockSpec((1, S, HEAD_DIM), lambda b, g: (b, 0, kcb + g)),
                  pl.BlockSpec((1, S, HEAD_DIM), lambda b, g: (b, 0, kcb + G + g)),
                  pl.BlockSpec(cmp_pos.shape, full3),
                  pl.BlockSpec(cmp_w1.shape, full3),
                  pl.BlockSpec(cmp_w2.shape, full3),
                  pl.BlockSpec((3, HEAD_DIM), full2),
                  pl.BlockSpec((n_seg, HEAD_DIM), full2),
                  pl.BlockSpec((n_seg, HEAD_DIM), full2),
                  pl.BlockSpec((n_seg, HEAD_DIM), full2)],
        out_specs=(pl.BlockSpec((1, 1, n_seg, HEAD_DIM), lambda b, g: (b, g, 0, 0)),
                   pl.BlockSpec((1, 1, HEAD_DIM, n_seg), lambda b, g: (b, g, 0, 0))),
        compiler_params=_cp("parallel", "parallel"),
        name="nsa_compress",
    )(proj, proj, cmp_pos, cmp_w1.astype(BF16), cmp_w2.astype(BF16), k_norm_g, *cmp_tables)


def _flash_step(s, m, l, acc, v_t):
    m_new = jnp.maximum(m, jnp.max(s, axis=0, keepdims=True))
    alpha = jnp.exp(m - m_new)
    p = jnp.exp(s - m_new)
    l_new = alpha * l + jnp.sum(p, axis=0, keepdims=True)
    acc_new = alpha * acc + _dot(v_t, p.astype(BF16))
    return m_new, l_new, acc_new


def _nsa_attn_kernel(q_ref, kc_ref, vct_ref, ks_ref, vst_ref, kw_ref, vwt_ref, gate_ref, ovl_ref, exp_ref,
                     o_ref, selm_ref):
    TQ, TK = ATTN_TQ, LANES
    R = NSA_HPG * TQ
    qi = pl.program_id(2)
    t0 = qi * TQ
    q = jnp.concatenate([q_ref[0, :, h * HEAD_DIM:(h + 1) * HEAD_DIM] for h in range(NSA_HPG)], axis=0)
    tq = t0 + (lax.broadcasted_iota(jnp.int32, (TK, R), 1) & (TQ - 1))
    key = lax.broadcasted_iota(jnp.int32, (TK, R), 0)

    s = _dot_nt(kc_ref[0, 0], q)
    mask = (key * CMP_STRIDE + (CMP_BLOCK - 1)) <= tq
    s = jnp.where(mask, s, NEG_BIG)
    p = jnp.where(mask, jnp.exp(s - jnp.max(s, axis=0, keepdims=True)), 0.0)
    l = jnp.sum(p, axis=0, keepdims=True)
    pb = (p / jnp.where(l > 0.0, l, 1.0)).astype(BF16)
    o_cmp = _dot(vct_ref[0, 0], pb)

    n_slc = ovl_ref.shape[0]
    imp = _dot(ovl_ref[...], pb[:, 0:TQ])
    for h in range(1, NSA_HPG):
        imp = imp + _dot(ovl_ref[...], pb[:, h * TQ:(h + 1) * TQ])
    t = t0 + lax.broadcasted_iota(jnp.int32, (n_slc, TQ), 1)
    blk = lax.broadcasted_iota(jnp.int32, (n_slc, TQ), 0)
    cur = t // SLC_BLOCK
    forced = (blk == 0) | (blk == cur) | (blk == cur - 1)
    valid = blk * SLC_BLOCK <= t
    val = jnp.where(forced, jnp.inf, jnp.where(valid, imp, -jnp.inf))
    rank = jnp.zeros((n_slc, TQ), F32)
    for i in range(n_slc):
        vi = val[i:i + 1, :]
        ahead = (vi > val) | ((vi == val) & (blk > i))
        rank = rank + jnp.where(ahead, 1.0, 0.0)
    sel = jnp.where((rank < float(SLC_TOPK)) & (val > -jnp.inf), 1.0, 0.0).astype(BF16)
    selm_ref[...] = _dot(exp_ref[...], sel)

    init = (jnp.full((1, R), M_INIT, F32), jnp.zeros((1, R), F32), jnp.zeros((HEAD_DIM, R), F32))

    def tile(k_ref, vt_ref, kt, carry, use_sel, causal, window):
        off = pl.multiple_of(kt * TK, TK)
        s = _dot_nt(k_ref[0, pl.ds(off, TK), :], q)
        mask = None
        if use_sel:
            sm = selm_ref[pl.ds(off, TK), :]
            mask = jnp.concatenate([sm] * NSA_HPG, axis=1) > 0.5
        if causal:
            c = (off + key) <= tq
            mask = c if mask is None else mask & c
        if window:
            w = (tq - (off + key)) < WINDOW
            mask = w if mask is None else mask & w
        if mask is not None:
            s = jnp.where(mask, s, NEG_BIG)
        return _flash_step(s, *carry, vt_ref[0, :, pl.ds(off, TK)])

    def finish(carry):
        _, l, acc = carry
        return acc / jnp.where(l > 0.0, l, 1.0)

    carry = lax.fori_loop(0, qi, lambda kt, c: tile(ks_ref, vst_ref, kt, c, True, False, False), init)
    o_slc = finish(tile(ks_ref, vst_ref, qi, carry, True, True, False))

    n_back = WINDOW // TK
    first = jnp.maximum(qi - n_back, 0)
    carry = lax.fori_loop(first, jnp.where(qi >= n_back, first + 1, first),
                          lambda kt, c: tile(kw_ref, vwt_ref, kt, c, False, False, True), init)
    carry = lax.fori_loop(jnp.maximum(qi - n_back + 1, 0), qi,
                          lambda kt, c: tile(kw_ref, vwt_ref, kt, c, False, False, False), carry)
    o_win = finish(tile(kw_ref, vwt_ref, qi, carry, False, True, False))

    gate = jax.nn.sigmoid(gate_ref[0, 0])
    for h in range(NSA_HPG):
        cols = slice(h * TQ, (h + 1) * TQ)
        o = (gate[3 * h:3 * h + 1, :] * o_cmp[:, cols] + gate[3 * h + 1:3 * h + 2, :] * o_slc[:, cols]
             + gate[3 * h + 2:3 * h + 3, :] * o_win[:, cols])
        o_ref[0, :, h * HEAD_DIM:(h + 1) * HEAD_DIM] = o.T.astype(o_ref.dtype)


def _nsa_attention(qn, kcmp, vcmp_t, ksn, vs_t, kwn, vw_t, gate_logits_t):
    B, S, _ = qn.shape
    G = NSA_KV_GROUPS
    TQ = ATTN_TQ
    n_cmp = S // CMP_STRIDE
    n_slc = S // SLC_BLOCK
    assert n_cmp == LANES and WINDOW % LANES == 0 and TQ == LANES
    cmp_start = np.arange(n_cmp) * CMP_STRIDE
    slc_start = np.arange(n_slc) * SLC_BLOCK
    overlap = np.clip(np.minimum(cmp_start[None, :] + CMP_BLOCK, slc_start[:, None] + SLC_BLOCK)
                      - np.maximum(cmp_start[None, :], slc_start[:, None]), 0, None) / CMP_STRIDE
    ovl = jnp.asarray(overlap, dtype=BF16)
    expand = jnp.asarray((np.arange(S)[:, None] // SLC_BLOCK) == np.arange(n_slc)[None, :], dtype=BF16)
    gq = NSA_HPG * HEAD_DIM
    q_spec = pl.BlockSpec((1, TQ, gq), lambda b, g, i: (b, i, g))
    k_spec = pl.BlockSpec((1, S, HEAD_DIM), lambda b, g, i: (b, 0, g))
    vt_spec = pl.BlockSpec((1, HEAD_DIM, S), lambda b, g, i: (b, g, 0))
    return pl.pallas_call(
        _nsa_attn_kernel,
        out_shape=jax.ShapeDtypeStruct((B, S, NSA_Q), BF16),
        grid=(B, G, S // TQ),
        in_specs=[q_spec,
                  pl.BlockSpec((1, 1, n_cmp, HEAD_DIM), lambda b, g, i: (b, g, 0, 0)),
                  pl.BlockSpec((1, 1, HEAD_DIM, n_cmp), lambda b, g, i: (b, g, 0, 0)),
                  k_spec, vt_spec, k_spec, vt_spec,
                  pl.BlockSpec((1, 1, 3 * NSA_HPG, TQ), lambda b, g, i: (b, g, 0, i)),
                  pl.BlockSpec(ovl.shape, lambda b, g, i: (0, 0)),
                  pl.BlockSpec(expand.shape, lambda b, g, i: (0, 0))],
        out_specs=q_spec,
        scratch_shapes=[pltpu.VMEM((S, TQ), F32)],
        compiler_params=_cp("parallel", "parallel", "parallel"),
        name="nsa_attention",
    )(qn, kcmp, vcmp_t, ksn, vs_t, kwn, vw_t, gate_logits_t, ovl, expand)


def _gla_kernel(q_ref, k_ref, v_ref, r_ref, a_ref, wa_ref, ba_ref, ng_ref, o_ref, state_ref, *, a_off):
    C = GLA_CHUNK

    @pl.when(pl.program_id(2) == 0)
    def _():
        state_ref[...] = jnp.zeros_like(state_ref)

    row = lax.broadcasted_iota(jnp.int32, (C, C), 0)
    colc = lax.broadcasted_iota(jnp.int32, (C, C), 1)
    causal = colc <= row
    tri = jnp.where(causal, 1.0, 0.0).astype(BF16)
    wa = wa_ref[...].astype(BF16)
    for c in range(q_ref.shape[1] // C):
        rows = slice(c * C, (c + 1) * C)
        a_low = a_ref[0, rows, a_off:a_off + GLA_RANK].astype(BF16)
        z = _dot(a_low, wa) + ba_ref[...]
        la = (jnp.minimum(z, 0.0) - jnp.log1p(jnp.exp(-jnp.abs(z)))) / GLA_TAU
        hi = la.astype(BF16)
        r1 = la - hi.astype(F32)
        mid = r1.astype(BF16)
        lo = (r1 - mid.astype(F32)).astype(BF16)
        bcum = _dot(tri, hi) + _dot(tri, mid) + _dot(tri, lo)
        blast = bcum[C - 1:C, :]
        kh = k_ref[0, rows, :]
        q_in = (q_ref[0, rows, :] * (GLA_DK ** -0.5) * jnp.exp(bcum)).astype(BF16)
        k_in = (kh * jnp.exp(-bcum)).astype(BF16)
        k_out = (kh * jnp.exp(blast - bcum)).astype(BF16)
        vb = v_ref[0, rows, :].astype(BF16)
        a_intra = jnp.where(causal, _dot_nt(q_in, k_in), 0.0).astype(BF16)
        state = state_ref[...]
        o = _dot(a_intra, vb) + _dot_nt(q_in, state.astype(BF16))
        state_ref[...] = state * jnp.exp(blast) + _dot_tn(vb, k_out)
        y = o * lax.rsqrt(jnp.mean(o * o, axis=-1, keepdims=True) + 1e-6) * ng_ref[...]
        o_ref[0, rows, :] = (y * _silu(r_ref[0, rows, :])).astype(o_ref.dtype)


def _gla(proj, proj_small, a_off, w_a2, b_a, norm_g, q_col):
    B, S, _ = proj.shape
    H, DK, DV, TC = GLA_HEADS, GLA_DK, GLA_DV, GLA_TC
    qb = q_col // DK
    kb = qb + H
    vb = (q_col + 2 * H * DK) // DV
    rb = vb + H
    return pl.pallas_call(
        functools.partial(_gla_kernel, a_off=a_off),
        out_shape=jax.ShapeDtypeStruct((B, S, H * DV), BF16),
        grid=(B, H, S // TC),
        in_specs=[pl.BlockSpec((1, TC, DK), lambda b, h, c: (b, c, qb + h)),
                  pl.BlockSpec((1, TC, DK), lambda b, h, c: (b, c, kb + h)),
                  pl.BlockSpec((1, TC, DV), lambda b, h, c: (b, c, vb + h)),
                  pl.BlockSpec((1, TC, DV), lambda b, h, c: (b, c, rb + h)),
                  pl.BlockSpec((1, TC, LANES), lambda b, h, c: (b, c, 0)),
                  pl.BlockSpec((GLA_RANK, DK), lambda b, h, c: (0, h)),
                  pl.BlockSpec((1, DK), lambda b, h, c: (0, h)),
                  pl.BlockSpec((1, DV), lambda b, h, c: (0, 0))],
        out_specs=pl.BlockSpec((1, TC, DV), lambda b, h, c: (b, c, h)),
        scratch_shapes=[pltpu.VMEM((DV, DK), F32)],
        compiler_params=_cp("parallel", "parallel", "arbitrary"),
        name="gla",
    )(proj, proj, proj, proj, proj_small, w_a2, b_a.reshape(1, H * DK), norm_g.reshape(1, DV))


def _conv_ln_kernel(u_ref, halo_ref, w_ref, b_ref, g_ref, beta_ref, o_ref, cat_ref, y_ref):
    TS, HALO, CH = CONV_TS, CONV_HALO, CONV_LANE_CHUNK
    D = u_ref.shape[2]
    n_chunks = D // CH
    first = HALO - (CONV_WIDTH - 1)
    ext = TS + SUBLANES

    @pl.when(pl.program_id(1) == 0)
    def _():
        cat_ref[0:HALO, :] = jnp.zeros((HALO, D), F32)

    @pl.when(pl.program_id(1) > 0)
    def _():
        cat_ref[0:HALO, :] = halo_ref[0]

    cat_ref[HALO:HALO + TS, :] = u_ref[0]
    cat_ref[HALO + TS:HALO + ext, :] = jnp.zeros((SUBLANES, D), F32)

    def conv_chunk(c, total):
        lanes = pl.ds(pl.multiple_of(c * CH, CH), CH)
        acc = jnp.zeros((TS, CH), F32) + b_ref[:, lanes]
        for r in range(SUBLANES):
            part = None
            for a in range((first + CONV_WIDTH - 1) // SUBLANES + 1):
                k = SUBLANES * a + r - first
                if 0 <= k < CONV_WIDTH:
                    term = cat_ref[pl.ds(SUBLANES * a, ext), lanes] * w_ref[pl.ds(k, 1), lanes]
                    part = term if part is None else part + term
            acc = acc + part[r:r + TS]
        y_ref[:, lanes] = acc
        return total + acc

    total = lax.fori_loop(0, n_chunks, conv_chunk, jnp.zeros((TS, CH), F32))
    mu = jnp.broadcast_to(jnp.sum(total, axis=-1, keepdims=True) / D, (TS, CH))

    def var_chunk(c, sq):
        d = y_ref[:, pl.ds(pl.multiple_of(c * CH, CH), CH)] - mu
        return sq + d * d

    sq = lax.fori_loop(0, n_chunks, var_chunk, jnp.zeros((TS, CH), F32))
    inv = jnp.broadcast_to(lax.rsqrt(jnp.sum(sq, axis=-1, keepdims=True) / D + 1e-5), (TS, CH))

    def out_chunk(c, carry):
        lanes = pl.ds(pl.multiple_of(c * CH, CH), CH)
        z = (y_ref[:, lanes] - mu) * inv * g_ref[:, lanes] + beta_ref[:, lanes]
        o_ref[0, :, lanes] = _silu(z).astype(o_ref.dtype)
        return carry

    lax.fori_loop(0, n_chunks, out_chunk, 0)


def _conv_ln_silu(u, w_dw, b_dw, ln_g, ln_b):
    B, S, D = u.shape
    TS, HALO = CONV_TS, CONV_HALO
    ratio = TS // HALO
    vec = pl.BlockSpec((1, D), lambda b, s: (0, 0))
    return pl.pallas_call(
        _conv_ln_kernel,
        out_shape=jax.ShapeDtypeStruct((B, S, D), BF16),
        grid=(B, S // TS),
        in_specs=[pl.BlockSpec((1, TS, D), lambda b, s: (b, s, 0)),
                  pl.BlockSpec((1, HALO, D), lambda b, s: (b, jnp.maximum(s * ratio - 1, 0), 0)),
                  pl.BlockSpec((CONV_WIDTH, D), lambda b, s: (0, 0)),
                  vec, vec, vec],
        out_specs=pl.BlockSpec((1, TS, D), lambda b, s: (b, s, 0)),
        scratch_shapes=[pltpu.VMEM((HALO + TS + SUBLANES, D), F32), pltpu.VMEM((TS, D), F32)],
        compiler_params=_cp("parallel", "parallel"),
        name="conv_ln_silu",
    )(u, u, w_dw, b_dw.reshape(1, D), ln_g.reshape(1, D), ln_b.reshape(1, D))


def _hybrid_attention(h, x2, gate, w_in, w_out, q_norm_g, k_norm_g, cmp_pos, cmp_w1, cmp_w2,
                      gla_w_a2, gla_b_a, gla_norm_g, B, S):
    D = h.shape[1]
    n_gate = NSA_HEADS * 3
    o_gl = NSA_Q + 6 * NSA_KV
    o_gq = o_gl + n_gate
    o_ga = o_gq + 2 * GLA_HEADS * GLA_DK + GLA_HEADS * GLA_DV
    o_gr = o_ga + GLA_RANK
    w_main = jnp.concatenate([w_in[:, :o_gl], w_in[:, o_gq:o_ga], w_in[:, o_gr:]], axis=1)
    w_small = jnp.concatenate([w_in[:, o_gl:o_gq], w_in[:, o_ga:o_gr],
                               jnp.zeros((D, LANES - n_gate - GLA_RANK), w_in.dtype)], axis=1)
    proj = _matmul(h, w_main, F32, 1024, 512).reshape(B, S, -1)
    proj_small = _matmul(h, w_small, F32, 1024, LANES).reshape(B, S, LANES)

    t = jnp.arange(S, dtype=jnp.int32)
    cmp_end = jnp.arange(S // CMP_STRIDE, dtype=jnp.int32) * CMP_STRIDE + (CMP_BLOCK - 1)
    qn, ksn, vs_t, kwn, vw_t = _nsa_prep(proj, _rope_tables(t), q_norm_g, k_norm_g)
    kcmp, vcmp_t = _nsa_compress(proj, cmp_pos, cmp_w1, cmp_w2, k_norm_g, _rope_tables(cmp_end))
    gate_logits_t = proj_small[:, :, :n_gate].reshape(B, S, NSA_KV_GROUPS, 3 * NSA_HPG).transpose(0, 2, 3, 1)
    o_nsa = _nsa_attention(qn, kcmp, vcmp_t, ksn, vs_t, kwn, vw_t, gate_logits_t)
    o_gla = _gla(proj, proj_small, n_gate, gla_w_a2, gla_b_a, gla_norm_g, o_gl)
    xs = (o_nsa.reshape(B * S, -1), o_gla.reshape(B * S, -1))
    return _matmul_residual(xs, w_out, jnp.zeros((1, D), F32), x2, gate, S, 1024, 512)


def _conformer(h, x2, gate, w_pw1, b_pw1, w_dw, b_dw, ln_g, ln_b, w_pw2, b_pw2, B, S):
    D = h.shape[1]
    tn = 256
    b1 = b_pw1.reshape(1, 2 * D)
    u = _matmul_glu(h, w_pw1, w_pw1, b1, b1, D // tn, False, F32, 1024, tn)
    v = _conv_ln_silu(u.reshape(B, S, D), w_dw, b_dw, ln_g, ln_b).reshape(B * S, D)
    return _matmul_residual((v,), w_pw2, b_pw2.reshape(1, D), x2, gate, S, 1024, 512)


def _swiglu(h, x2, gate, w_gate, w_up, w_down, S):
    D = h.shape[1]
    F = w_gate.shape[1]
    zeros = jnp.zeros((1, F), F32)
    act = _matmul_glu(h, w_gate, w_up, zeros, zeros, 0, True, BF16, 1024, 256)
    return _matmul_residual((act,), w_down, jnp.zeros((1, D), F32), x2, gate, S, 512, 256)


def kernel(x, c, w_mod, b_mod, ada_table, norm_mix_g, norm_ffn_g, w_in, w_out, q_norm_g, k_norm_g, cmp_pos, cmp_w1, cmp_w2, gla_w_a2, gla_b_a, gla_norm_g, cv_w_pw1, cv_b_pw1, cv_w_dw, cv_b_dw, cv_ln_g, cv_ln_b, cv_w_pw2, cv_b_pw2, ffn_w_gate, ffn_w_up, ffn_w_down):
    B, S, D = x.shape
    depth = ada_table.shape[0]
    mod = _ada_mod(c, w_mod, b_mod).reshape(B, N_MOD, D)
    for layer in range(depth):
        m = mod + ada_table[layer]
        sh_a, sc_a, g_a, sh_f, sc_f, g_f = [m[:, i, :] for i in range(N_MOD)]
        h = _norm_mod(x, norm_mix_g[layer], sc_a, sh_a).reshape(B * S, D)
        x2 = x.reshape(B * S, D)
        g_a3 = g_a.reshape(B, 1, D)
        j = layer // 2
        if layer % 2 == 0:
            x2 = _hybrid_attention(h, x2, g_a3, w_in[j], w_out[j], q_norm_g[j], k_norm_g[j], cmp_pos[j],
                                   cmp_w1[j], cmp_w2[j], gla_w_a2[j], gla_b_a[j], gla_norm_g[j], B, S)
        else:
            x2 = _conformer(h, x2, g_a3, cv_w_pw1[j], cv_b_pw1[j], cv_w_dw[j], cv_b_dw[j], cv_ln_g[j],
                            cv_ln_b[j], cv_w_pw2[j], cv_b_pw2[j], B, S)
        x = x2.reshape(B, S, D)
        h = _norm_mod(x, norm_ffn_g[layer], sc_f, sh_f).reshape(B * S, D)
        x2 = _swiglu(h, x2, g_f.reshape(B, 1, D), ffn_w_gate[layer], ffn_w_up[layer], ffn_w_down[layer], S)
        x = x2.reshape(B, S, D)
    return x
```

```python
import functools

import numpy as np
import jax
import jax.numpy as jnp
from jax import lax
from jax.experimental import pallas as pl
from jax.experimental.pallas import tpu as pltpu

F32 = jnp.float32
BF16 = jnp.bfloat16

HEAD_DIM = 128
NSA_HEADS = 16
NSA_KV_GROUPS = 4
NSA_HPG = NSA_HEADS // NSA_KV_GROUPS
CMP_BLOCK = 32
CMP_STRIDE = 16
SLC_BLOCK = 64
SLC_TOPK = 16
WINDOW = 512
ROPE_THETA = 500000.0
ROPE_DIM = HEAD_DIM // 4
GLA_HEADS = 4
GLA_DK = 256
GLA_DV = 512
GLA_RANK = 16
GLA_TAU = 16.0
GLA_CHUNK = 64
CONV_WIDTH = 31
N_MOD = 6
NSA_Q = NSA_HEADS * HEAD_DIM
NSA_KV = NSA_KV_GROUPS * HEAD_DIM

VMEM_LIMIT_BYTES = 56 * 1024 * 1024
LANES = 128
SUBLANES = 8
NEG_BIG = -1e30
M_INIT = -1e29

ATTN_TQ = 128
ATTN_UNROLL = 4
CONV_TS = 128
CONV_HALO = 32
CONV_LANE_CHUNK = 128
GLA_TC = 256


def _cp(*sem):
    return pltpu.CompilerParams(dimension_semantics=sem, vmem_limit_bytes=VMEM_LIMIT_BYTES)


def _dot(a, b):
    return jnp.dot(a, b, preferred_element_type=F32)


def _dot_nt(a, b):
    return lax.dot_general(a, b, (((1,), (1,)), ((), ())), preferred_element_type=F32)


def _dot_tn(a, b):
    return lax.dot_general(a, b, (((0,), (0,)), ((), ())), preferred_element_type=F32)


def _silu(x):
    return x * jax.nn.sigmoid(x)


def _cast_weights_once(w_refs, wbf_refs):
    @pl.when(pl.program_id(1) == 0)
    def _():
        for w_ref, wbf_ref in zip(w_refs, wbf_refs):
            wbf_ref[...] = w_ref[...].astype(BF16)


def _mm_kernel(x_ref, w_ref, o_ref):
    o_ref[...] = _dot(x_ref[...], w_ref[...]).astype(o_ref.dtype)


def _matmul(x, w, out_dtype, tm, tn):
    M, K = x.shape
    N = w.shape[1]
    return pl.pallas_call(
        _mm_kernel,
        out_shape=jax.ShapeDtypeStruct((M, N), out_dtype),
        grid=(M // tm, N // tn),
        in_specs=[pl.BlockSpec((tm, K), lambda i, j: (i, 0)),
                  pl.BlockSpec((K, tn), lambda i, j: (0, j))],
        out_specs=pl.BlockSpec((tm, tn), lambda i, j: (i, j)),
        compiler_params=_cp("parallel", "parallel"),
        name="matmul",
    )(x, w)


def _mm_glu_kernel(x_ref, wa_ref, wb_ref, ba_ref, bb_ref, o_ref, wabf_ref, wbbf_ref, *, act_on_a):
    _cast_weights_once((wa_ref, wb_ref), (wabf_ref, wbbf_ref))
    x = x_ref[...]
    a = _dot(x, wabf_ref[...]) + ba_ref[...]
    b = _dot(x, wbbf_ref[...]) + bb_ref[...]
    y = _silu(a) * b if act_on_a else a * jax.nn.sigmoid(b)
    o_ref[...] = y.astype(o_ref.dtype)


def _matmul_glu(x, wa, wb, ba, bb, b_off, act_on_a, out_dtype, tm, tn):
    M, K = x.shape
    N = ba.shape[1] - b_off * tn if wa is wb else ba.shape[1]
    return pl.pallas_call(
        functools.partial(_mm_glu_kernel, act_on_a=act_on_a),
        out_shape=jax.ShapeDtypeStruct((M, N), out_dtype),
        grid=(N // tn, M // tm),
        in_specs=[pl.BlockSpec((tm, K), lambda j, i: (i, 0)),
                  pl.BlockSpec((K, tn), lambda j, i: (0, j)),
                  pl.BlockSpec((K, tn), lambda j, i: (0, j + b_off)),
                  pl.BlockSpec((1, tn), lambda j, i: (0, j)),
                  pl.BlockSpec((1, tn), lambda j, i: (0, j + b_off))],
        out_specs=pl.BlockSpec((tm, tn), lambda j, i: (i, j)),
        scratch_shapes=[pltpu.VMEM((K, tn), BF16), pltpu.VMEM((K, tn), BF16)],
        compiler_params=_cp("arbitrary", "arbitrary"),
        name="matmul_glu",
    )(x, wa, wb, ba, bb)


def _mm_res_kernel(*refs, n_x, cast):
    x_refs = refs[:n_x]
    if cast:
        w_ref, b_ref, res_ref, g_ref, o_ref, wbf_ref = refs[n_x:]
        _cast_weights_once((w_ref,), (wbf_ref,))
    else:
        wbf_ref, b_ref, res_ref, g_ref, o_ref = refs[n_x:]
    y = b_ref[...]
    off = 0
    for x_ref in x_refs:
        k = x_ref.shape[1]
        y = y + _dot(x_ref[...], wbf_ref[off:off + k, :])
        off += k
    o_ref[...] = res_ref[...] + g_ref[0] * y


def _matmul_residual(xs, w, bias, res, gate, rows_per_batch, tm, tn):
    M = xs[0].shape[0]
    K, N = w.shape
    bpt = rows_per_batch // tm
    cast = w.dtype != BF16
    if cast:
        grid = (N // tn, M // tm)
        ij = lambda f: (lambda j, i: f(i, j))
        x_mode, w_mode = None, pl.Buffered(1)
        scratch = [pltpu.VMEM((K, tn), BF16)]
    else:
        grid = (M // tm, N // tn)
        ij = lambda f: f
        x_mode, w_mode = pl.Buffered(1), None
        scratch = []
    return pl.pallas_call(
        functools.partial(_mm_res_kernel, n_x=len(xs), cast=cast),
        out_shape=jax.ShapeDtypeStruct((M, N), F32),
        grid=grid,
        in_specs=[pl.BlockSpec((tm, x.shape[1]), ij(lambda i, j: (i, 0)), pipeline_mode=x_mode) for x in xs] + [
                  pl.BlockSpec((K, tn), ij(lambda i, j: (0, j)), pipeline_mode=w_mode),
                  pl.BlockSpec((1, tn), ij(lambda i, j: (0, j))),
                  pl.BlockSpec((tm, tn), ij(lambda i, j: (i, j))),
                  pl.BlockSpec((1, 1, tn), ij(lambda i, j: (i // bpt, 0, j)))],
        out_specs=pl.BlockSpec((tm, tn), ij(lambda i, j: (i, j))),
        scratch_shapes=scratch,
        compiler_params=_cp("arbitrary", "arbitrary"),
        name="matmul_residual",
    )(*xs, w, bias, res, gate)


def _cast_kernel(w_ref, o_ref):
    o_ref[...] = w_ref[...].astype(o_ref.dtype)


def _cast_bf16(w, rows):
    K, N = w.shape
    return pl.pallas_call(
        _cast_kernel,
        out_shape=jax.ShapeDtypeStruct((K, N), BF16),
        grid=(K // rows,),
        in_specs=[pl.BlockSpec((rows, N), lambda i: (i, 0))],
        out_specs=pl.BlockSpec((rows, N), lambda i: (i, 0)),
        compiler_params=_cp("parallel"),
        name="cast_bf16",
    )(w)


def _regroup_kernel(w_ref, main_ref, small_ref, *, segments, small_segments):
    off = 0
    for lo, hi in segments:
        main_ref[:, off:off + hi - lo] = w_ref[:, lo:hi].astype(BF16)
        off += hi - lo
    small_ref[...] = jnp.zeros(small_ref.shape, BF16)
    off = 0
    for lo, hi in small_segments:
        small_ref[:, off:off + hi - lo] = w_ref[:, lo:hi].astype(BF16)
        off += hi - lo


def _regroup_w_in(w_in, segments, small_segments, rows):
    K, N = w_in.shape
    n_main = sum(hi - lo for lo, hi in segments)
    return pl.pallas_call(
        functools.partial(_regroup_kernel, segments=segments, small_segments=small_segments),
        out_shape=(jax.ShapeDtypeStruct((K, n_main), BF16), jax.ShapeDtypeStruct((K, LANES), BF16)),
        grid=(K // rows,),
        in_specs=[pl.BlockSpec((rows, N), lambda i: (i, 0))],
        out_specs=(pl.BlockSpec((rows, n_main), lambda i: (i, 0)), pl.BlockSpec((rows, LANES), lambda i: (i, 0))),
        compiler_params=_cp("parallel"),
        name="regroup_w_in",
    )(w_in)


def _mod_kernel(c_ref, w_ref, b_ref, o_ref):
    a = _silu(c_ref[...]).astype(BF16)
    o_ref[...] = _dot(a, w_ref[...].astype(BF16)) + b_ref[...]


def _ada_mod(c, w_mod, b_mod):
    B, D = c.shape
    N = w_mod.shape[1]
    tn = 512
    return pl.pallas_call(
        _mod_kernel,
        out_shape=jax.ShapeDtypeStruct((B, N), F32),
        grid=(N // tn,),
        in_specs=[pl.BlockSpec((B, D), lambda j: (0, 0)),
                  pl.BlockSpec((D, tn), lambda j: (0, j)),
                  pl.BlockSpec((1, tn), lambda j: (0, j))],
        out_specs=pl.BlockSpec((B, tn), lambda j: (0, j)),
        compiler_params=_cp("parallel"),
        name="ada_mod",
    )(c, w_mod, b_mod.reshape(1, N))


def _norm_mod_kernel(x_ref, g_ref, sc_ref, sh_ref, o_ref):
    x = x_ref[0]
    y = x * lax.rsqrt(jnp.mean(x * x, axis=-1, keepdims=True) + 1e-6)
    o_ref[0] = ((y * g_ref[...]) * (1.0 + sc_ref[0]) + sh_ref[0]).astype(o_ref.dtype)


def _norm_mod(x, g, sc, sh):
    B, S, D = x.shape
    ts = 256
    return pl.pallas_call(
        _norm_mod_kernel,
        out_shape=jax.ShapeDtypeStruct((B, S, D), BF16),
        grid=(B, S // ts),
        in_specs=[pl.BlockSpec((1, ts, D), lambda b, s: (b, s, 0)),
                  pl.BlockSpec((1, D), lambda b, s: (0, 0)),
                  pl.BlockSpec((1, 1, D), lambda b, s: (b, 0, 0)),
                  pl.BlockSpec((1, 1, D), lambda b, s: (b, 0, 0))],
        out_specs=pl.BlockSpec((1, ts, D), lambda b, s: (b, s, 0)),
        compiler_params=_cp("parallel", "parallel"),
        name="norm_mod",
    )(x, g.reshape(1, D), sc.reshape(B, 1, D), sh.reshape(B, 1, D))


def _rope_tables(pos):
    half = ROPE_DIM // 2
    inv_freq = ROPE_THETA ** (-jnp.arange(half, dtype=F32) / half)
    ang = pos.astype(F32)[:, None] * inv_freq[None, :]
    cos, sin = jnp.cos(ang), jnp.sin(ang)
    n = pos.shape[0]
    rest = HEAD_DIM - ROPE_DIM
    c = jnp.concatenate([cos, cos, jnp.ones((n, rest), F32)], axis=-1)
    s_lo = jnp.concatenate([-sin, jnp.zeros((n, HEAD_DIM - half), F32)], axis=-1)
    s_hi = jnp.concatenate([jnp.zeros((n, half), F32), sin, jnp.zeros((n, rest), F32)], axis=-1)
    return c, s_lo, s_hi


def _norm_rope(x, g, c, s_lo, s_hi):
    y = x * lax.rsqrt(jnp.mean(x * x, axis=-1, keepdims=True) + 1e-6) * g
    half = ROPE_DIM // 2
    return y * c + pltpu.roll(y, HEAD_DIM - half, 1) * s_lo + pltpu.roll(y, half, 1) * s_hi


def _nsa_prep_kernel(q_ref, ks_ref, vs_ref, kw_ref, vw_ref, c_ref, slo_ref, shi_ref, qg_ref, kg_ref,
                     qo_ref, kso_ref, vso_ref, kwo_ref, vwo_ref):
    c, s_lo, s_hi = c_ref[...], slo_ref[...], shi_ref[...]
    scale = HEAD_DIM ** -0.5
    for h in range(NSA_HEADS):
        sl = slice(h * HEAD_DIM, (h + 1) * HEAD_DIM)
        qo_ref[0, :, sl] = (_norm_rope(q_ref[0, :, sl], qg_ref[...], c, s_lo, s_hi) * scale).astype(BF16)
    for g in range(NSA_KV_GROUPS):
        sl = slice(g * HEAD_DIM, (g + 1) * HEAD_DIM)
        kso_ref[0, :, sl] = _norm_rope(ks_ref[0, :, sl], kg_ref[1:2, :], c, s_lo, s_hi).astype(BF16)
        kwo_ref[0, :, sl] = _norm_rope(kw_ref[0, :, sl], kg_ref[2:3, :], c, s_lo, s_hi).astype(BF16)
    vso_ref[0] = vs_ref[0].T.astype(BF16)
    vwo_ref[0] = vw_ref[0].T.astype(BF16)


def _nsa_prep(proj, tables, q_norm_g, k_norm_g):
    B, S, _ = proj.shape
    ts = 256
    kvb = NSA_Q // NSA_KV

    def kv_spec(n):
        return pl.BlockSpec((1, ts, NSA_KV), lambda b, s: (b, s, kvb + n))

    tab = pl.BlockSpec((ts, HEAD_DIM), lambda b, s: (s, 0))
    out_k = pl.BlockSpec((1, ts, NSA_KV), lambda b, s: (b, s, 0))
    out_vt = pl.BlockSpec((1, NSA_KV, ts), lambda b, s: (b, 0, s))
    k_shape = jax.ShapeDtypeStruct((B, S, NSA_KV), BF16)
    vt_shape = jax.ShapeDtypeStruct((B, NSA_KV, S), BF16)
    return pl.pallas_call(
        _nsa_prep_kernel,
        out_shape=(jax.ShapeDtypeStruct((B, S, NSA_Q), BF16), k_shape, vt_shape, k_shape, vt_shape),
        grid=(B, S // ts),
        in_specs=[pl.BlockSpec((1, ts, NSA_Q), lambda b, s: (b, s, 0)),
                  kv_spec(2), kv_spec(3), kv_spec(4), kv_spec(5), tab, tab, tab,
                  pl.BlockSpec((1, HEAD_DIM), lambda b, s: (0, 0)),
                  pl.BlockSpec((3, HEAD_DIM), lambda b, s: (0, 0))],
        out_specs=(pl.BlockSpec((1, ts, NSA_Q), lambda b, s: (b, s, 0)), out_k, out_vt, out_k, out_vt),
        compiler_params=_cp("parallel", "parallel"),
        name="nsa_prep",
    )(proj, proj, proj, proj, proj, *tables, q_norm_g.reshape(1, HEAD_DIM), k_norm_g)


def _compress_kernel(kc_ref, vc_ref, pos_ref, w1_ref, w2_ref, kg_ref, c_ref, slo_ref, shi_ref,
                     ko_ref, vo_ref):
    half_blk = CMP_BLOCK // 2
    n_seg = kc_ref.shape[1] // CMP_STRIDE

    def compress(tok_ref, j):
        u = jnp.zeros((n_seg, w1_ref.shape[2]), F32)
        v = jnp.zeros((n_seg, w1_ref.shape[2]), F32)
        for l in range(half_blk):
            x = tok_ref[0, pl.ds(l, n_seg, stride=CMP_STRIDE), :]
            xa = (x + pos_ref[j, l:l + 1, :]).astype(BF16)
            xb = (x + pos_ref[j, half_blk + l:half_blk + l + 1, :]).astype(BF16)
            u = u + _dot(xa, w1_ref[j, l * HEAD_DIM:(l + 1) * HEAD_DIM, :])
            v = v + _dot(xb, w1_ref[j, (half_blk + l) * HEAD_DIM:(half_blk + l + 1) * HEAD_DIM, :])
        h = u + pltpu.roll(v, n_seg - 1, 0)
        return _dot(jax.nn.gelu(h).astype(BF16), w2_ref[j])

    k = compress(kc_ref, 0)
    ko_ref[0, 0] = _norm_rope(k, kg_ref[0:1, :], c_ref[...], slo_ref[...], shi_ref[...]).astype(BF16)
    vo_ref[0, 0] = compress(vc_ref, 1).T.astype(BF16)


def _nsa_compress(proj, cmp_pos, cmp_w1, cmp_w2, k_norm_g, cmp_tables):
    B, S, _ = proj.shape
    G = NSA_KV_GROUPS
    n_seg = S // CMP_STRIDE
    kcb = NSA_Q // HEAD_DIM
    full2 = lambda b, g: (0, 0)
    full3 = lambda b, g: (0, 0, 0)
    return pl.pallas_call(
        _compress_kernel,
        out_shape=(jax.ShapeDtypeStruct((B, G, n_seg, HEAD_DIM), BF16),
                   jax.ShapeDtypeStruct((B, G, HEAD_DIM, n_seg), BF16)),
        grid=(B, G),
        in_specs=[pl.BlockSpec((1, S, HEAD_DIM), lambda b, g: (b, 0, kcb + g)),
                  pl.BlockSpec((1, S, HEAD_DIM), lambda b, g: (b, 0, kcb + G + g)),
                  pl.BlockSpec(cmp_pos.shape, full3),
                  pl.BlockSpec(cmp_w1.shape, full3),
                  pl.BlockSpec(cmp_w2.shape, full3),
                  pl.BlockSpec((3, HEAD_DIM), full2),
                  pl.BlockSpec((n_seg, HEAD_DIM), full2),
                  pl.BlockSpec((n_seg, HEAD_DIM), full2),
                  pl.BlockSpec((n_seg, HEAD_DIM), full2)],
        out_specs=(pl.BlockSpec((1, 1, n_seg, HEAD_DIM), lambda b, g: (b, g, 0, 0)),
                   pl.BlockSpec((1, 1, HEAD_DIM, n_seg), lambda b, g: (b, g, 0, 0))),
        compiler_params=_cp("parallel", "parallel"),
        name="nsa_compress",
    )(proj, proj, cmp_pos, cmp_w1.astype(BF16), cmp_w2.astype(BF16), k_norm_g, *cmp_tables)


def _unrolled_loop(lo, hi, body, carry, unroll):
    n_main = (hi - lo) // unroll

    def main(i, c):
        for u in range(unroll):
            c = body(lo + i * unroll + u, c)
        return c

    carry = lax.fori_loop(0, n_main, main, carry)
    return lax.fori_loop(lo + n_main * unroll, hi, body, carry)


def _fold_rows(x, op):
    return op(x.reshape(x.shape[0] // SUBLANES, SUBLANES, x.shape[1]), axis=0)


def _nsa_attn_kernel(q_ref, kc_ref, vct_ref, ks_ref, vst_ref, kw_ref, vwt_ref, gate_ref, ovl_ref, exp_ref,
                     o_ref, selm_ref, ss_ref, sw_ref):
    TQ, TK = ATTN_TQ, LANES
    R = NSA_HPG * TQ
    qi = pl.program_id(2)
    t0 = qi * TQ
    q = jnp.concatenate([q_ref[0, :, h * HEAD_DIM:(h + 1) * HEAD_DIM] for h in range(NSA_HPG)], axis=0)
    tq = t0 + (lax.broadcasted_iota(jnp.int32, (TK, R), 1) & (TQ - 1))
    key = lax.broadcasted_iota(jnp.int32, (TK, R), 0)

    s = _dot_nt(kc_ref[0, 0], q)
    mask = (key * CMP_STRIDE + (CMP_BLOCK - 1)) <= tq
    s = jnp.where(mask, s, NEG_BIG)
    p = jnp.where(mask, jnp.exp(s - jnp.max(s, axis=0, keepdims=True)), 0.0)
    l = jnp.sum(p, axis=0, keepdims=True)
    pb = (p / jnp.where(l > 0.0, l, 1.0)).astype(BF16)
    o_cmp = _dot(vct_ref[0, 0], pb)

    n_slc = ovl_ref.shape[0]
    ranked = t0 + TQ > SLC_TOPK * SLC_BLOCK

    @pl.when(ranked)
    def _():
        imp = _dot(ovl_ref[...], pb[:, 0:TQ])
        for h in range(1, NSA_HPG):
            imp = imp + _dot(ovl_ref[...], pb[:, h * TQ:(h + 1) * TQ])
        t = t0 + lax.broadcasted_iota(jnp.int32, (n_slc, TQ), 1)
        blk = lax.broadcasted_iota(jnp.int32, (n_slc, TQ), 0)
        cur = t // SLC_BLOCK
        forced = (blk == 0) | (blk == cur) | (blk == cur - 1)
        valid = blk * SLC_BLOCK <= t
        val = jnp.where(forced, jnp.inf, jnp.where(valid, imp, -jnp.inf))
        rank = jnp.zeros((n_slc, TQ), F32)
        for i in range(n_slc):
            vi = val[i:i + 1, :]
            ahead = (vi > val) | ((vi == val) & (blk > i))
            rank = rank + jnp.where(ahead, 1.0, 0.0)
        sel = jnp.where((rank < float(SLC_TOPK)) & (val > -jnp.inf), 1.0, 0.0).astype(BF16)
        selm_ref[...] = _dot(exp_ref[...], sel)

    @pl.when(jnp.logical_not(ranked))
    def _():
        selm_ref[pl.ds(pl.multiple_of(t0, TQ), TQ), :] = jnp.ones((TQ, TQ), F32)

    def score_tile(k_ref, scr_ref, slot, kt, m8, use_sel, causal, window):
        off = pl.multiple_of(kt * TK, TK)
        s = _dot_nt(k_ref[0, pl.ds(off, TK), :], q)
        mask = None
        if use_sel:
            sm = selm_ref[pl.ds(off, TK), :]
            mask = jnp.concatenate([sm] * NSA_HPG, axis=1) > 0.5
        if causal:
            c = (off + key) <= tq
            mask = c if mask is None else mask & c
        if window:
            w = (tq - (off + key)) < WINDOW
            mask = w if mask is None else mask & w
        if mask is not None:
            s = jnp.where(mask, s, NEG_BIG)
        scr_ref[slot] = s
        return jnp.maximum(m8, _fold_rows(s, jnp.max))

    def value_tile(scr_ref, vt_ref, slot, kt, m, carry):
        l8, acc = carry
        off = pl.multiple_of(kt * TK, TK)
        p = jnp.exp(scr_ref[slot] - m)
        return l8 + _fold_rows(p, jnp.sum), acc + _dot(vt_ref[0, :, pl.ds(off, TK)], p.astype(BF16))

    def finish(carry):
        l8, acc = carry
        l = jnp.sum(l8, axis=0, keepdims=True)
        return acc / jnp.where(l > 0.0, l, 1.0)

    m_init = jnp.full((SUBLANES, R), M_INIT, F32)
    acc_init = (jnp.zeros((SUBLANES, R), F32), jnp.zeros((HEAD_DIM, R), F32))

    n_masked = jnp.where(ranked, qi, 0)
    m8 = _unrolled_loop(0, n_masked, lambda kt, m: score_tile(ks_ref, ss_ref, kt, kt, m, True, False, False),
                        m_init, ATTN_UNROLL)
    m8_s = _unrolled_loop(0, qi - n_masked, lambda kt, m: score_tile(ks_ref, ss_ref, kt, kt, m, False, False, False),
                          m8, ATTN_UNROLL)
    n_back = WINDOW // TK
    first = jnp.maximum(qi - n_back, 0)
    uncut = jnp.where(qi >= n_back, first + 1, first)
    m8 = lax.fori_loop(first, uncut,
                       lambda kt, m: score_tile(kw_ref, sw_ref, kt - first, kt, m, False, False, True), m_init)
    m8_w = _unrolled_loop(uncut, qi, lambda kt, m: score_tile(kw_ref, sw_ref, kt - first, kt, m, False, False, False),
                          m8, n_back - 1)
    m8_s = score_tile(ks_ref, ss_ref, qi, qi, m8_s, True, True, False)
    m8_w = score_tile(kw_ref, sw_ref, qi - first, qi, m8_w, False, True, False)
    m_s = jnp.max(m8_s, axis=0, keepdims=True)
    m_w = jnp.max(m8_w, axis=0, keepdims=True)

    c_s = _unrolled_loop(0, qi, lambda kt, c: value_tile(ss_ref, vst_ref, kt, kt, m_s, c), acc_init, ATTN_UNROLL)
    c_w = _unrolled_loop(first, qi, lambda kt, c: value_tile(sw_ref, vwt_ref, kt - first, kt, m_w, c),
                         acc_init, n_back)
    o_slc = finish(value_tile(ss_ref, vst_ref, qi, qi, m_s, c_s))
    o_win = finish(value_tile(sw_ref, vwt_ref, qi - first, qi, m_w, c_w))

    gate = jax.nn.sigmoid(gate_ref[0, 0])
    for h in range(NSA_HPG):
        cols = slice(h * TQ, (h + 1) * TQ)
        o = (gate[3 * h:3 * h + 1, :] * o_cmp[:, cols] + gate[3 * h + 1:3 * h + 2, :] * o_slc[:, cols]
             + gate[3 * h + 2:3 * h + 3, :] * o_win[:, cols])
        o_ref[0, :, h * HEAD_DIM:(h + 1) * HEAD_DIM] = o.T.astype(o_ref.dtype)


def _nsa_attention(qn, kcmp, vcmp_t, ksn, vs_t, kwn, vw_t, gate_logits_t):
    B, S, _ = qn.shape
    G = NSA_KV_GROUPS
    TQ = ATTN_TQ
    n_cmp = S // CMP_STRIDE
    n_slc = S // SLC_BLOCK
    assert n_cmp == LANES and WINDOW % LANES == 0 and TQ == LANES
    cmp_start = np.arange(n_cmp) * CMP_STRIDE
    slc_start = np.arange(n_slc) * SLC_BLOCK
    overlap = np.clip(np.minimum(cmp_start[None, :] + CMP_BLOCK, slc_start[:, None] + SLC_BLOCK)
                      - np.maximum(cmp_start[None, :], slc_start[:, None]), 0, None) / CMP_STRIDE
    ovl = jnp.asarray(overlap, dtype=BF16)
    expand = jnp.asarray((np.arange(S)[:, None] // SLC_BLOCK) == np.arange(n_slc)[None, :], dtype=BF16)
    gq = NSA_HPG * HEAD_DIM
    q_spec = pl.BlockSpec((1, TQ, gq), lambda b, g, i: (b, i, g))
    k_spec = pl.BlockSpec((1, S, HEAD_DIM), lambda b, g, i: (b, 0, g))
    vt_spec = pl.BlockSpec((1, HEAD_DIM, S), lambda b, g, i: (b, g, 0))
    return pl.pallas_call(
        _nsa_attn_kernel,
        out_shape=jax.ShapeDtypeStruct((B, S, NSA_Q), BF16),
        grid=(B, G, S // TQ),
        in_specs=[q_spec,
                  pl.BlockSpec((1, 1, n_cmp, HEAD_DIM), lambda b, g, i: (b, g, 0, 0)),
                  pl.BlockSpec((1, 1, HEAD_DIM, n_cmp), lambda b, g, i: (b, g, 0, 0)),
                  k_spec, vt_spec, k_spec, vt_spec,
                  pl.BlockSpec((1, 1, 3 * NSA_HPG, TQ), lambda b, g, i: (b, g, 0, i)),
                  pl.BlockSpec(ovl.shape, lambda b, g, i: (0, 0)),
                  pl.BlockSpec(expand.shape, lambda b, g, i: (0, 0))],
        out_specs=q_spec,
        scratch_shapes=[pltpu.VMEM((S, TQ), F32),
                        pltpu.VMEM((S // LANES, LANES, NSA_HPG * TQ), F32),
                        pltpu.VMEM((WINDOW // LANES + 1, LANES, NSA_HPG * TQ), F32)],
        compiler_params=_cp("parallel", "parallel", "parallel"),
        name="nsa_attention",
    )(qn, kcmp, vcmp_t, ksn, vs_t, kwn, vw_t, gate_logits_t, ovl, expand)


def _gla_kernel(q_ref, k_ref, v_ref, r_ref, a_ref, wa_ref, ba_ref, ng_ref, o_ref, state_ref, *, a_off):
    C = GLA_CHUNK

    @pl.when(pl.program_id(2) == 0)
    def _():
        state_ref[...] = jnp.zeros_like(state_ref)

    row = lax.broadcasted_iota(jnp.int32, (C, C), 0)
    colc = lax.broadcasted_iota(jnp.int32, (C, C), 1)
    causal = colc <= row
    tri = jnp.where(causal, 1.0, 0.0).astype(BF16)
    wa = wa_ref[...].astype(BF16)
    for c in range(q_ref.shape[1] // C):
        rows = slice(c * C, (c + 1) * C)
        a_low = a_ref[0, rows, a_off:a_off + GLA_RANK].astype(BF16)
        z = _dot(a_low, wa) + ba_ref[...]
        la = (jnp.minimum(z, 0.0) - jnp.log1p(jnp.exp(-jnp.abs(z)))) / GLA_TAU
        hi = la.astype(BF16)
        r1 = la - hi.astype(F32)
        mid = r1.astype(BF16)
        lo = (r1 - mid.astype(F32)).astype(BF16)
        bcum = _dot(tri, hi) + _dot(tri, mid) + _dot(tri, lo)
        blast = bcum[C - 1:C, :]
        kh = k_ref[0, rows, :]
        q_in = (q_ref[0, rows, :] * (GLA_DK ** -0.5) * jnp.exp(bcum)).astype(BF16)
        k_in = (kh * jnp.exp(-bcum)).astype(BF16)
        k_out = (kh * jnp.exp(blast - bcum)).astype(BF16)
        vb = v_ref[0, rows, :].astype(BF16)
        a_intra = jnp.where(causal, _dot_nt(q_in, k_in), 0.0).astype(BF16)
        state = state_ref[...]
        o = _dot(a_intra, vb) + _dot_nt(q_in, state.astype(BF16))
        state_ref[...] = state * jnp.exp(blast) + _dot_tn(vb, k_out)
        y = o * lax.rsqrt(jnp.mean(o * o, axis=-1, keepdims=True) + 1e-6) * ng_ref[...]
        o_ref[0, rows, :] = (y * _silu(r_ref[0, rows, :])).astype(o_ref.dtype)


def _gla(proj, proj_small, a_off, w_a2, b_a, norm_g, q_col):
    B, S, _ = proj.shape
    H, DK, DV, TC = GLA_HEADS, GLA_DK, GLA_DV, GLA_TC
    qb = q_col // DK
    kb = qb + H
    vb = (q_col + 2 * H * DK) // DV
    rb = vb + H
    return pl.pallas_call(
        functools.partial(_gla_kernel, a_off=a_off),
        out_shape=jax.ShapeDtypeStruct((B, S, H * DV), BF16),
        grid=(B, H, S // TC),
        in_specs=[pl.BlockSpec((1, TC, DK), lambda b, h, c: (b, c, qb + h)),
                  pl.BlockSpec((1, TC, DK), lambda b, h, c: (b, c, kb + h)),
                  pl.BlockSpec((1, TC, DV), lambda b, h, c: (b, c, vb + h)),
                  pl.BlockSpec((1, TC, DV), lambda b, h, c: (b, c, rb + h)),
                  pl.BlockSpec((1, TC, LANES), lambda b, h, c: (b, c, 0)),
                  pl.BlockSpec((GLA_RANK, DK), lambda b, h, c: (0, h)),
                  pl.BlockSpec((1, DK), lambda b, h, c: (0, h)),
                  pl.BlockSpec((1, DV), lambda b, h, c: (0, 0))],
        out_specs=pl.BlockSpec((1, TC, DV), lambda b, h, c: (b, c, h)),
        scratch_shapes=[pltpu.VMEM((DV, DK), F32)],
        compiler_params=_cp("parallel", "parallel", "arbitrary"),
        name="gla",
    )(proj, proj, proj, proj, proj_small, w_a2, b_a.reshape(1, H * DK), norm_g.reshape(1, DV))


def _conv_ln_kernel(u_ref, halo_ref, w_ref, b_ref, g_ref, beta_ref, o_ref, cat_ref, y_ref):
    TS, HALO, CH = CONV_TS, CONV_HALO, CONV_LANE_CHUNK
    D = u_ref.shape[2]
    n_chunks = D // CH
    first = HALO - (CONV_WIDTH - 1)
    ext = TS + SUBLANES

    @pl.when(pl.program_id(1) == 0)
    def _():
        cat_ref[0:HALO, :] = jnp.zeros((HALO, D), F32)

    @pl.when(pl.program_id(1) > 0)
    def _():
        cat_ref[0:HALO, :] = halo_ref[0]

    cat_ref[HALO:HALO + TS, :] = u_ref[0]
    cat_ref[HALO + TS:HALO + ext, :] = jnp.zeros((SUBLANES, D), F32)

    def conv_chunk(c, total):
        lanes = pl.ds(pl.multiple_of(c * CH, CH), CH)
        acc = jnp.zeros((TS, CH), F32) + b_ref[:, lanes]
        for r in range(SUBLANES):
            part = None
            for a in range((first + CONV_WIDTH - 1) // SUBLANES + 1):
                k = SUBLANES * a + r - first
                if 0 <= k < CONV_WIDTH:
                    term = cat_ref[pl.ds(SUBLANES * a, ext), lanes] * w_ref[pl.ds(k, 1), lanes]
                    part = term if part is None else part + term
            acc = acc + part[r:r + TS]
        y_ref[:, lanes] = acc
        return total + acc

    total = lax.fori_loop(0, n_chunks, conv_chunk, jnp.zeros((TS, CH), F32))
    mu = jnp.broadcast_to(jnp.sum(total, axis=-1, keepdims=True) / D, (TS, CH))

    def var_chunk(c, sq):
        d = y_ref[:, pl.ds(pl.multiple_of(c * CH, CH), CH)] - mu
        return sq + d * d

    sq = lax.fori_loop(0, n_chunks, var_chunk, jnp.zeros((TS, CH), F32))
    inv = jnp.broadcast_to(lax.rsqrt(jnp.sum(sq, axis=-1, keepdims=True) / D + 1e-5), (TS, CH))

    def out_chunk(c, carry):
        lanes = pl.ds(pl.multiple_of(c * CH, CH), CH)
        z = (y_ref[:, lanes] - mu) * inv * g_ref[:, lanes] + beta_ref[:, lanes]
        o_ref[0, :, lanes] = _silu(z).astype(o_ref.dtype)
        return carry

    lax.fori_loop(0, n_chunks, out_chunk, 0)


def _conv_ln_silu(u, w_dw, b_dw, ln_g, ln_b):
    B, S, D = u.shape
    TS, HALO = CONV_TS, CONV_HALO
    ratio = TS // HALO
    vec = pl.BlockSpec((1, D), lambda b, s: (0, 0))
    return pl.pallas_call(
        _conv_ln_kernel,
        out_shape=jax.ShapeDtypeStruct((B, S, D), BF16),
        grid=(B, S // TS),
        in_specs=[pl.BlockSpec((1, TS, D), lambda b, s: (b, s, 0)),
                  pl.BlockSpec((1, HALO, D), lambda b, s: (b, jnp.maximum(s * ratio - 1, 0), 0)),
                  pl.BlockSpec((CONV_WIDTH, D), lambda b, s: (0, 0)),
                  vec, vec, vec],
        out_specs=pl.BlockSpec((1, TS, D), lambda b, s: (b, s, 0)),
        scratch_shapes=[pltpu.VMEM((HALO + TS + SUBLANES, D), F32), pltpu.VMEM((TS, D), F32)],
        compiler_params=_cp("parallel", "parallel"),
        name="conv_ln_silu",
    )(u, u, w_dw, b_dw.reshape(1, D), ln_g.reshape(1, D), ln_b.reshape(1, D))


def _hybrid_attention(h, x2, gate, w_in, w_out, q_norm_g, k_norm_g, cmp_pos, cmp_w1, cmp_w2,
                      gla_w_a2, gla_b_a, gla_norm_g, B, S):
    D = h.shape[1]
    n_gate = NSA_HEADS * 3
    o_gl = NSA_Q + 6 * NSA_KV
    o_gq = o_gl + n_gate
    o_ga = o_gq + 2 * GLA_HEADS * GLA_DK + GLA_HEADS * GLA_DV
    o_gr = o_ga + GLA_RANK
    w_main, w_small = _regroup_w_in(w_in, ((0, o_gl), (o_gq, o_ga), (o_gr, w_in.shape[1])),
                                    ((o_gl, o_gq), (o_ga, o_gr)), 128)
    proj = _matmul(h, w_main, F32, 1024, 512).reshape(B, S, -1)
    proj_small = _matmul(h, w_small, F32, 1024, LANES).reshape(B, S, LANES)

    t = jnp.arange(S, dtype=jnp.int32)
    cmp_end = jnp.arange(S // CMP_STRIDE, dtype=jnp.int32) * CMP_STRIDE + (CMP_BLOCK - 1)
    qn, ksn, vs_t, kwn, vw_t = _nsa_prep(proj, _rope_tables(t), q_norm_g, k_norm_g)
    kcmp, vcmp_t = _nsa_compress(proj, cmp_pos, cmp_w1, cmp_w2, k_norm_g, _rope_tables(cmp_end))
    gate_logits_t = proj_small[:, :, :n_gate].reshape(B, S, NSA_KV_GROUPS, 3 * NSA_HPG).transpose(0, 2, 3, 1)
    o_nsa = _nsa_attention(qn, kcmp, vcmp_t, ksn, vs_t, kwn, vw_t, gate_logits_t)
    o_gla = _gla(proj, proj_small, n_gate, gla_w_a2, gla_b_a, gla_norm_g, o_gl)
    xs = (o_nsa.reshape(B * S, -1), o_gla.reshape(B * S, -1))
    return _matmul_residual(xs, w_out, jnp.zeros((1, D), F32), x2, gate, S, 1024, 512)


def _conformer(h, x2, gate, w_pw1, b_pw1, w_dw, b_dw, ln_g, ln_b, w_pw2, b_pw2, B, S):
    D = h.shape[1]
    tn = 256
    b1 = b_pw1.reshape(1, 2 * D)
    u = _matmul_glu(h, w_pw1, w_pw1, b1, b1, D // tn, False, F32, 1024, tn)
    v = _conv_ln_silu(u.reshape(B, S, D), w_dw, b_dw, ln_g, ln_b).reshape(B * S, D)
    return _matmul_residual((v,), w_pw2, b_pw2.reshape(1, D), x2, gate, S, 1024, 512)


def _swiglu(h, x2, gate, w_gate, w_up, w_down, S):
    D = h.shape[1]
    F = w_gate.shape[1]
    zeros = jnp.zeros((1, F), F32)
    act = _matmul_glu(h, w_gate, w_up, zeros, zeros, 0, True, BF16, 1024, 256)
    return _matmul_residual((act,), _cast_bf16(w_down, 256), jnp.zeros((1, D), F32), x2, gate, S, 1024, 256)


def kernel(x, c, w_mod, b_mod, ada_table, norm_mix_g, norm_ffn_g, w_in, w_out, q_norm_g, k_norm_g, cmp_pos, cmp_w1, cmp_w2, gla_w_a2, gla_b_a, gla_norm_g, cv_w_pw1, cv_b_pw1, cv_w_dw, cv_b_dw, cv_ln_g, cv_ln_b, cv_w_pw2, cv_b_pw2, ffn_w_gate, ffn_w_up, ffn_w_down):
    B, S, D = x.shape
    depth = ada_table.shape[0]
    mod = _ada_mod(c, w_mod, b_mod).reshape(B, N_MOD, D)
    for layer in range(depth):
        m = mod + ada_table[layer]
        sh_a, sc_a, g_a, sh_f, sc_f, g_f = [m[:, i, :] for i in range(N_MOD)]
        h = _norm_mod(x, norm_mix_g[layer], sc_a, sh_a).reshape(B * S, D)
        x2 = x.reshape(B * S, D)
        g_a3 = g_a.reshape(B, 1, D)
        j = layer // 2
        if layer % 2 == 0:
            x2 = _hybrid_attention(h, x2, g_a3, w_in[j], w_out[j], q_norm_g[j], k_norm_g[j], cmp_pos[j],
                                   cmp_w1[j], cmp_w2[j], gla_w_a2[j], gla_b_a[j], gla_norm_g[j], B, S)
        else:
            x2 = _conformer(h, x2, g_a3, cv_w_pw1[j], cv_b_pw1[j], cv_w_dw[j], cv_b_dw[j], cv_ln_g[j],
                            cv_ln_b[j], cv_w_pw2[j], cv_b_pw2[j], B, S)
        x = x2.reshape(B, S, D)
        h = _norm_mod(x, norm_ffn_g[layer], sc_f, sh_f).reshape(B * S, D)
        x2 = _swiglu(h, x2, g_f.reshape(B, 1, D), ffn_w_gate[layer], ffn_w_up[layer], ffn_w_down[layer], S)
        x = x2.reshape(B, S, D)
    return x
```

```python
import functools

import numpy as np
import jax
import jax.numpy as jnp
from jax import lax
from jax.experimental import pallas as pl
from jax.experimental.pallas import tpu as pltpu

F32 = jnp.float32
BF16 = jnp.bfloat16

HEAD_DIM = 128
NSA_HEADS = 16
NSA_KV_GROUPS = 4
NSA_HPG = NSA_HEADS // NSA_KV_GROUPS
CMP_BLOCK = 32
CMP_STRIDE = 16
SLC_BLOCK = 64
SLC_TOPK = 16
WINDOW = 512
ROPE_THETA = 500000.0
ROPE_DIM = HEAD_DIM // 4
GLA_HEADS = 4
GLA_DK = 256
GLA_DV = 512
GLA_RANK = 16
GLA_TAU = 16.0
GLA_CHUNK = 64
CONV_WIDTH = 31
N_MOD = 6
NSA_Q = NSA_HEADS * HEAD_DIM
NSA_KV = NSA_KV_GROUPS * HEAD_DIM

VMEM_LIMIT_BYTES = 56 * 1024 * 1024
LANES = 128
SUBLANES = 8
NEG_BIG = -1e30
M_INIT = -1e29

ATTN_TQ = 128
ATTN_UNROLL = 4
CONV_TS = 128
CONV_HALO = 32
CONV_LANE_CHUNK = 128
GLA_TC = 128


def _cp(*sem):
    return pltpu.CompilerParams(dimension_semantics=sem, vmem_limit_bytes=VMEM_LIMIT_BYTES)


def _dot(a, b):
    return jnp.dot(a, b, preferred_element_type=F32)


def _dot_nt(a, b):
    return lax.dot_general(a, b, (((1,), (1,)), ((), ())), preferred_element_type=F32)


def _dot_tn(a, b):
    return lax.dot_general(a, b, (((0,), (0,)), ((), ())), preferred_element_type=F32)


def _silu(x):
    return x * jax.nn.sigmoid(x)


def _cast_weights_once(w_refs, wbf_refs):
    @pl.when(pl.program_id(1) == 0)
    def _():
        for w_ref, wbf_ref in zip(w_refs, wbf_refs):
            wbf_ref[...] = w_ref[...].astype(BF16)


def _mm_kernel(x_ref, w_ref, o_ref):
    o_ref[...] = _dot(x_ref[...], w_ref[...]).astype(o_ref.dtype)


def _matmul(x, w, out_dtype, tm, tn):
    M, K = x.shape
    N = w.shape[1]
    return pl.pallas_call(
        _mm_kernel,
        out_shape=jax.ShapeDtypeStruct((M, N), out_dtype),
        grid=(M // tm, N // tn),
        in_specs=[pl.BlockSpec((tm, K), lambda i, j: (i, 0)),
                  pl.BlockSpec((K, tn), lambda i, j: (0, j))],
        out_specs=pl.BlockSpec((tm, tn), lambda i, j: (i, j)),
        compiler_params=_cp("parallel", "parallel"),
        name="matmul",
    )(x, w)


def _mm_glu_kernel(x_ref, wa_ref, wb_ref, ba_ref, bb_ref, o_ref, wabf_ref, wbbf_ref, *, act_on_a):
    _cast_weights_once((wa_ref, wb_ref), (wabf_ref, wbbf_ref))
    x = x_ref[...]
    a = _dot(x, wabf_ref[...]) + ba_ref[...]
    b = _dot(x, wbbf_ref[...]) + bb_ref[...]
    y = _silu(a) * b if act_on_a else a * jax.nn.sigmoid(b)
    o_ref[...] = y.astype(o_ref.dtype)


def _matmul_glu(x, wa, wb, layer, ba, bb, b_off, act_on_a, out_dtype, tm, tn):
    M, K = x.shape
    N = ba.shape[1] - b_off * tn if wa is wb else ba.shape[1]
    return pl.pallas_call(
        functools.partial(_mm_glu_kernel, act_on_a=act_on_a),
        out_shape=jax.ShapeDtypeStruct((M, N), out_dtype),
        grid=(N // tn, M // tm),
        in_specs=[pl.BlockSpec((tm, K), lambda j, i: (i, 0)),
                  pl.BlockSpec((None, K, tn), lambda j, i: (layer, 0, j)),
                  pl.BlockSpec((None, K, tn), lambda j, i: (layer, 0, j + b_off)),
                  pl.BlockSpec((1, tn), lambda j, i: (0, j)),
                  pl.BlockSpec((1, tn), lambda j, i: (0, j + b_off))],
        out_specs=pl.BlockSpec((tm, tn), lambda j, i: (i, j)),
        scratch_shapes=[pltpu.VMEM((K, tn), BF16), pltpu.VMEM((K, tn), BF16)],
        compiler_params=_cp("arbitrary", "arbitrary"),
        name="matmul_glu",
    )(x, wa, wb, ba, bb)


def _mm_res_kernel(*refs, n_x, cast):
    x_refs = refs[:n_x]
    if cast:
        w_ref, b_ref, res_ref, g_ref, o_ref, wbf_ref = refs[n_x:]
        _cast_weights_once((w_ref,), (wbf_ref,))
    else:
        wbf_ref, b_ref, res_ref, g_ref, o_ref = refs[n_x:]
    y = b_ref[...]
    off = 0
    for x_ref in x_refs:
        k = x_ref.shape[1]
        y = y + _dot(x_ref[...], wbf_ref[off:off + k, :])
        off += k
    o_ref[...] = res_ref[...] + g_ref[0] * y


def _matmul_residual(xs, w, layer, bias, res, gate, rows_per_batch, tm, tn):
    M = xs[0].shape[0]
    K, N = w.shape[-2:]
    bpt = rows_per_batch // tm
    cast = w.ndim == 3
    w_block = (None, K, tn) if cast else (K, tn)
    w_index = (lambda i, j: (layer, 0, j)) if cast else (lambda i, j: (0, j))
    if cast:
        grid = (N // tn, M // tm)
        ij = lambda f: (lambda j, i: f(i, j))
        x_mode, w_mode = None, pl.Buffered(1)
        scratch = [pltpu.VMEM((K, tn), BF16)]
    else:
        grid = (M // tm, N // tn)
        ij = lambda f: f
        x_mode, w_mode = pl.Buffered(1), None
        scratch = []
    return pl.pallas_call(
        functools.partial(_mm_res_kernel, n_x=len(xs), cast=cast),
        out_shape=jax.ShapeDtypeStruct((M, N), F32),
        grid=grid,
        in_specs=[pl.BlockSpec((tm, x.shape[1]), ij(lambda i, j: (i, 0)), pipeline_mode=x_mode) for x in xs] + [
                  pl.BlockSpec(w_block, ij(w_index), pipeline_mode=w_mode),
                  pl.BlockSpec((1, tn), ij(lambda i, j: (0, j))),
                  pl.BlockSpec((tm, tn), ij(lambda i, j: (i, j))),
                  pl.BlockSpec((1, 1, tn), ij(lambda i, j: (i // bpt, 0, j)))],
        out_specs=pl.BlockSpec((tm, tn), ij(lambda i, j: (i, j))),
        scratch_shapes=scratch,
        compiler_params=_cp("arbitrary", "arbitrary"),
        name="matmul_residual",
    )(*xs, w, bias, res, gate)


def _cast_kernel(w_ref, o_ref):
    o_ref[...] = w_ref[...].astype(o_ref.dtype)


def _cast_bf16(w, layer, rows):
    _, K, N = w.shape
    return pl.pallas_call(
        _cast_kernel,
        out_shape=jax.ShapeDtypeStruct((K, N), BF16),
        grid=(K // rows,),
        in_specs=[pl.BlockSpec((None, rows, N), lambda i: (layer, i, 0))],
        out_specs=pl.BlockSpec((rows, N), lambda i: (i, 0)),
        compiler_params=_cp("parallel"),
        name="cast_bf16",
    )(w)


def _regroup_kernel(w_ref, main_ref, small_ref, *, segments, small_segments):
    off = 0
    for lo, hi in segments:
        main_ref[:, off:off + hi - lo] = w_ref[:, lo:hi].astype(BF16)
        off += hi - lo
    small_ref[...] = jnp.zeros(small_ref.shape, BF16)
    off = 0
    for lo, hi in small_segments:
        small_ref[:, off:off + hi - lo] = w_ref[:, lo:hi].astype(BF16)
        off += hi - lo


def _regroup_w_in(w_in, layer, segments, small_segments, rows):
    _, K, N = w_in.shape
    n_main = sum(hi - lo for lo, hi in segments)
    return pl.pallas_call(
        functools.partial(_regroup_kernel, segments=segments, small_segments=small_segments),
        out_shape=(jax.ShapeDtypeStruct((K, n_main), BF16), jax.ShapeDtypeStruct((K, LANES), BF16)),
        grid=(K // rows,),
        in_specs=[pl.BlockSpec((None, rows, N), lambda i: (layer, i, 0))],
        out_specs=(pl.BlockSpec((rows, n_main), lambda i: (i, 0)), pl.BlockSpec((rows, LANES), lambda i: (i, 0))),
        compiler_params=_cp("parallel"),
        name="regroup_w_in",
    )(w_in)


def _mod_kernel(c_ref, w_ref, b_ref, o_ref):
    a = _silu(c_ref[...]).astype(BF16)
    o_ref[...] = _dot(a, w_ref[...].astype(BF16)) + b_ref[...]


def _ada_mod(c, w_mod, b_mod):
    B, D = c.shape
    N = w_mod.shape[1]
    tn = 512
    return pl.pallas_call(
        _mod_kernel,
        out_shape=jax.ShapeDtypeStruct((B, N), F32),
        grid=(N // tn,),
        in_specs=[pl.BlockSpec((B, D), lambda j: (0, 0)),
                  pl.BlockSpec((D, tn), lambda j: (0, j)),
                  pl.BlockSpec((1, tn), lambda j: (0, j))],
        out_specs=pl.BlockSpec((B, tn), lambda j: (0, j)),
        compiler_params=_cp("parallel"),
        name="ada_mod",
    )(c, w_mod, b_mod.reshape(1, N))


def _norm_mod_kernel(x_ref, g_ref, sc_ref, sh_ref, o_ref, gm_ref):
    rows_per_trip = 2 * SUBLANES
    gm_ref[...] = g_ref[...] * (1.0 + sc_ref[0])

    def trip(i, carry):
        rows = pl.ds(pl.multiple_of(i * rows_per_trip, rows_per_trip), rows_per_trip)
        x = x_ref[0, rows, :]
        r = lax.rsqrt(jnp.mean(x * x, axis=-1, keepdims=True) + 1e-6)
        o_ref[0, rows, :] = (x * r * gm_ref[...] + sh_ref[0]).astype(o_ref.dtype)
        return carry

    lax.fori_loop(0, x_ref.shape[1] // rows_per_trip, trip, 0, unroll=4)


def _norm_mod(x, g, sc, sh):
    B, S, D = x.shape
    ts = 512
    return pl.pallas_call(
        _norm_mod_kernel,
        out_shape=jax.ShapeDtypeStruct((B, S, D), BF16),
        grid=(B, S // ts),
        in_specs=[pl.BlockSpec((1, ts, D), lambda b, s: (b, s, 0)),
                  pl.BlockSpec((1, D), lambda b, s: (0, 0)),
                  pl.BlockSpec((1, 1, D), lambda b, s: (b, 0, 0)),
                  pl.BlockSpec((1, 1, D), lambda b, s: (b, 0, 0))],
        out_specs=pl.BlockSpec((1, ts, D), lambda b, s: (b, s, 0)),
        scratch_shapes=[pltpu.VMEM((1, D), F32)],
        compiler_params=_cp("parallel", "parallel"),
        name="norm_mod",
    )(x, g.reshape(1, D), sc.reshape(B, 1, D), sh.reshape(B, 1, D))


def _rope_tables(pos):
    half = ROPE_DIM // 2
    inv_freq = ROPE_THETA ** (-jnp.arange(half, dtype=F32) / half)
    ang = pos.astype(F32)[:, None] * inv_freq[None, :]
    cos, sin = jnp.cos(ang), jnp.sin(ang)
    n = pos.shape[0]
    rest = HEAD_DIM - ROPE_DIM
    c = jnp.concatenate([cos, cos, jnp.ones((n, rest), F32)], axis=-1)
    s_lo = jnp.concatenate([-sin, jnp.zeros((n, HEAD_DIM - half), F32)], axis=-1)
    s_hi = jnp.concatenate([jnp.zeros((n, half), F32), sin, jnp.zeros((n, rest), F32)], axis=-1)
    return c, s_lo, s_hi


def _norm_rope(x, g, c, s_lo, s_hi):
    y = x * lax.rsqrt(jnp.mean(x * x, axis=-1, keepdims=True) + 1e-6) * g
    half = ROPE_DIM // 2
    return y * c + pltpu.roll(y, HEAD_DIM - half, 1) * s_lo + pltpu.roll(y, half, 1) * s_hi


def _nsa_prep_kernel(q_ref, ks_ref, vs_ref, kw_ref, vw_ref, c_ref, slo_ref, shi_ref, qg_ref, kg_ref,
                     qo_ref, kso_ref, vso_ref, kwo_ref, vwo_ref):
    c, s_lo, s_hi = c_ref[...], slo_ref[...], shi_ref[...]
    scale = HEAD_DIM ** -0.5
    for h in range(NSA_HEADS):
        sl = slice(h * HEAD_DIM, (h + 1) * HEAD_DIM)
        qo_ref[0, :, sl] = (_norm_rope(q_ref[0, :, sl], qg_ref[...], c, s_lo, s_hi) * scale).astype(BF16)
    for g in range(NSA_KV_GROUPS):
        sl = slice(g * HEAD_DIM, (g + 1) * HEAD_DIM)
        kso_ref[0, :, sl] = _norm_rope(ks_ref[0, :, sl], kg_ref[1:2, :], c, s_lo, s_hi).astype(BF16)
        kwo_ref[0, :, sl] = _norm_rope(kw_ref[0, :, sl], kg_ref[2:3, :], c, s_lo, s_hi).astype(BF16)
    vso_ref[0] = vs_ref[0].T.astype(BF16)
    vwo_ref[0] = vw_ref[0].T.astype(BF16)


def _nsa_prep(proj, tables, q_norm_g, k_norm_g):
    B, S, _ = proj.shape
    ts = 256
    kvb = NSA_Q // NSA_KV

    def kv_spec(n):
        return pl.BlockSpec((1, ts, NSA_KV), lambda b, s: (b, s, kvb + n))

    tab = pl.BlockSpec((ts, HEAD_DIM), lambda b, s: (s, 0))
    out_k = pl.BlockSpec((1, ts, NSA_KV), lambda b, s: (b, s, 0))
    out_vt = pl.BlockSpec((1, NSA_KV, ts), lambda b, s: (b, 0, s))
    k_shape = jax.ShapeDtypeStruct((B, S, NSA_KV), BF16)
    vt_shape = jax.ShapeDtypeStruct((B, NSA_KV, S), BF16)
    return pl.pallas_call(
        _nsa_prep_kernel,
        out_shape=(jax.ShapeDtypeStruct((B, S, NSA_Q), BF16), k_shape, vt_shape, k_shape, vt_shape),
        grid=(B, S // ts),
        in_specs=[pl.BlockSpec((1, ts, NSA_Q), lambda b, s: (b, s, 0)),
                  kv_spec(2), kv_spec(3), kv_spec(4), kv_spec(5), tab, tab, tab,
                  pl.BlockSpec((1, HEAD_DIM), lambda b, s: (0, 0)),
                  pl.BlockSpec((3, HEAD_DIM), lambda b, s: (0, 0))],
        out_specs=(pl.BlockSpec((1, ts, NSA_Q), lambda b, s: (b, s, 0)), out_k, out_vt, out_k, out_vt),
        compiler_params=_cp("parallel", "parallel"),
        name="nsa_prep",
    )(proj, proj, proj, proj, proj, *tables, q_norm_g.reshape(1, HEAD_DIM), k_norm_g)


def _compress_kernel(kc_ref, vc_ref, pos_ref, w1_ref, w2_ref, kg_ref, c_ref, slo_ref, shi_ref,
                     ko_ref, vo_ref):
    half_blk = CMP_BLOCK // 2
    n_seg = kc_ref.shape[1] // CMP_STRIDE

    def compress(tok_ref, j):
        u = jnp.zeros((n_seg, w1_ref.shape[2]), F32)
        v = jnp.zeros((n_seg, w1_ref.shape[2]), F32)
        for l in range(half_blk):
            x = tok_ref[0, pl.ds(l, n_seg, stride=CMP_STRIDE), :]
            xa = (x + pos_ref[j, l:l + 1, :]).astype(BF16)
            xb = (x + pos_ref[j, half_blk + l:half_blk + l + 1, :]).astype(BF16)
            u = u + _dot(xa, w1_ref[j, l * HEAD_DIM:(l + 1) * HEAD_DIM, :])
            v = v + _dot(xb, w1_ref[j, (half_blk + l) * HEAD_DIM:(half_blk + l + 1) * HEAD_DIM, :])
        h = u + pltpu.roll(v, n_seg - 1, 0)
        return _dot(jax.nn.gelu(h).astype(BF16), w2_ref[j])

    k = compress(kc_ref, 0)
    ko_ref[0, 0] = _norm_rope(k, kg_ref[0:1, :], c_ref[...], slo_ref[...], shi_ref[...]).astype(BF16)
    vo_ref[0, 0] = compress(vc_ref, 1).T.astype(BF16)


def _nsa_compress(proj, cmp_pos, cmp_w1, cmp_w2, k_norm_g, cmp_tables):
    B, S, _ = proj.shape
    G = NSA_KV_GROUPS
    n_seg = S // CMP_STRIDE
    kcb = NSA_Q // HEAD_DIM
    full2 = lambda b, g: (0, 0)
    full3 = lambda b, g: (0, 0, 0)
    return pl.pallas_call(
        _compress_kernel,
        out_shape=(jax.ShapeDtypeStruct((B, G, n_seg, HEAD_DIM), BF16),
                   jax.ShapeDtypeStruct((B, G, HEAD_DIM, n_seg), BF16)),
        grid=(B, G),
        in_specs=[pl.BlockSpec((1, S, HEAD_DIM), lambda b, g: (b, 0, kcb + g)),
                  pl.BlockSpec((1, S, HEAD_DIM), lambda b, g: (b, 0, kcb + G + g)),
                  pl.BlockSpec(cmp_pos.shape, full3),
                  pl.BlockSpec(cmp_w1.shape, full3),
                  pl.BlockSpec(cmp_w2.shape, full3),
                  pl.BlockSpec((3, HEAD_DIM), full2),
                  pl.BlockSpec((n_seg, HEAD_DIM), full2),
                  pl.BlockSpec((n_seg, HEAD_DIM), full2),
                  pl.BlockSpec((n_seg, HEAD_DIM), full2)],
        out_specs=(pl.BlockSpec((1, 1, n_seg, HEAD_DIM), lambda b, g: (b, g, 0, 0)),
                   pl.BlockSpec((1, 1, HEAD_DIM, n_seg), lambda b, g: (b, g, 0, 0))),
        compiler_params=_cp("parallel", "parallel"),
        name="nsa_compress",
    )(proj, proj, cmp_pos, cmp_w1.astype(BF16), cmp_w2.astype(BF16), k_norm_g, *cmp_tables)


def _unrolled_loop(lo, hi, body, carry, unroll):
    while unroll > 1:
        n_trips = (hi - lo) // unroll

        def trip(i, c, lo=lo, unroll=unroll):
            for u in range(unroll):
                c = body(lo + i * unroll + u, c)
            return c

        carry = lax.fori_loop(0, n_trips, trip, carry)
        lo = lo + n_trips * unroll
        unroll //= 2
    return lax.fori_loop(lo, hi, body, carry)


def _fold_rows(x, op):
    return op(x.reshape(x.shape[0] // SUBLANES, SUBLANES, x.shape[1]), axis=0)


def _nsa_attn_kernel(q_ref, kc_ref, vct_ref, ks_ref, vst_ref, kw_ref, vwt_ref, gate_ref, ovl_ref, exp_ref,
                     o_ref, selm_ref, ss_ref, sw_ref):
    TQ, TK = ATTN_TQ, LANES
    R = NSA_HPG * TQ
    qi = pl.program_id(2)
    t0 = qi * TQ
    q = jnp.concatenate([q_ref[0, :, h * HEAD_DIM:(h + 1) * HEAD_DIM] for h in range(NSA_HPG)], axis=0)
    tq = t0 + (lax.broadcasted_iota(jnp.int32, (TK, R), 1) & (TQ - 1))
    key = lax.broadcasted_iota(jnp.int32, (TK, R), 0)

    def score_tile(k_ref, scr_ref, slot, kt, m8, use_sel, causal, window):
        off = pl.multiple_of(kt * TK, TK)
        s = _dot_nt(k_ref[0, pl.ds(off, TK), :], q)
        mask = None
        if use_sel:
            sm = selm_ref[pl.ds(off, TK), :]
            mask = jnp.concatenate([sm] * NSA_HPG, axis=1) > 0.5
        if causal:
            c = (off + key) <= tq
            mask = c if mask is None else mask & c
        if window:
            w = (tq - (off + key)) < WINDOW
            mask = w if mask is None else mask & w
        if mask is not None:
            s = jnp.where(mask, s, NEG_BIG)
        scr_ref[slot] = s
        return jnp.maximum(m8, _fold_rows(s, jnp.max))

    def value_tile(scr_ref, vt_ref, slot, kt, m, carry):
        l8, acc = carry
        off = pl.multiple_of(kt * TK, TK)
        p = jnp.exp(scr_ref[slot] - m)
        return l8 + _fold_rows(p, jnp.sum), acc + _dot(vt_ref[0, :, pl.ds(off, TK)], p.astype(BF16))

    def finish(carry):
        l8, acc = carry
        l = jnp.sum(l8, axis=0, keepdims=True)
        return acc / jnp.where(l > 0.0, l, 1.0)

    m_init = jnp.full((SUBLANES, R), M_INIT, F32)
    acc_init = (jnp.zeros((SUBLANES, R), F32), jnp.zeros((HEAD_DIM, R), F32))

    n_back = WINDOW // TK
    first = jnp.maximum(qi - n_back, 0)
    m8_w = score_tile(kw_ref, sw_ref, qi - first, qi, m_init, False, True, False)

    s = _dot_nt(kc_ref[0, 0], q)
    mask = (key * CMP_STRIDE + (CMP_BLOCK - 1)) <= tq
    s = jnp.where(mask, s, NEG_BIG)
    p = jnp.where(mask, jnp.exp(s - jnp.max(s, axis=0, keepdims=True)), 0.0)
    l = jnp.sum(p, axis=0, keepdims=True)
    pb = (p / jnp.where(l > 0.0, l, 1.0)).astype(BF16)
    o_cmp = _dot(vct_ref[0, 0], pb)

    n_slc = ovl_ref.shape[0]
    ranked = t0 + TQ > SLC_TOPK * SLC_BLOCK

    @pl.when(ranked)
    def _():
        imp_heads = _dot(ovl_ref[...], pb)
        imp = imp_heads[:, 0:TQ]
        for h in range(1, NSA_HPG):
            imp = imp + imp_heads[:, h * TQ:(h + 1) * TQ]
        t = t0 + lax.broadcasted_iota(jnp.int32, (n_slc, TQ), 1)
        blk = lax.broadcasted_iota(jnp.int32, (n_slc, TQ), 0)
        cur = t // SLC_BLOCK
        forced = (blk == 0) | (blk == cur) | (blk == cur - 1)
        valid = blk * SLC_BLOCK <= t
        val = jnp.where(forced, jnp.inf, jnp.where(valid, imp, -jnp.inf))
        rank = jnp.zeros((n_slc, TQ), F32)
        for i in range(n_slc):
            vi = val[i:i + 1, :]
            ahead = (vi > val) | ((vi == val) & (blk > i))
            rank = rank + jnp.where(ahead, 1.0, 0.0)
        sel = jnp.where((rank < float(SLC_TOPK)) & (val > -jnp.inf), 1.0, 0.0).astype(BF16)
        selm_ref[...] = _dot(exp_ref[...], sel)

    @pl.when(jnp.logical_not(ranked))
    def _():
        selm_ref[pl.ds(pl.multiple_of(t0, TQ), TQ), :] = jnp.ones((TQ, TQ), F32)

    n_masked = jnp.where(ranked, qi, 0)
    m8 = _unrolled_loop(0, n_masked, lambda kt, m: score_tile(ks_ref, ss_ref, kt, kt, m, True, False, False),
                        m_init, ATTN_UNROLL)
    m8_s = _unrolled_loop(0, qi - n_masked, lambda kt, m: score_tile(ks_ref, ss_ref, kt, kt, m, False, False, False),
                          m8, ATTN_UNROLL)
    uncut = jnp.where(qi >= n_back, first + 1, first)
    m8 = lax.fori_loop(first, uncut,
                       lambda kt, m: score_tile(kw_ref, sw_ref, kt - first, kt, m, False, False, True), m8_w)
    m8_w = _unrolled_loop(uncut, qi, lambda kt, m: score_tile(kw_ref, sw_ref, kt - first, kt, m, False, False, False),
                          m8, n_back - 1)
    m8_s = score_tile(ks_ref, ss_ref, qi, qi, m8_s, True, True, False)
    m_s = jnp.max(m8_s, axis=0, keepdims=True)
    m_w = jnp.max(m8_w, axis=0, keepdims=True)

    c_s = _unrolled_loop(0, qi, lambda kt, c: value_tile(ss_ref, vst_ref, kt, kt, m_s, c), acc_init, ATTN_UNROLL)
    c_w = _unrolled_loop(first, qi, lambda kt, c: value_tile(sw_ref, vwt_ref, kt - first, kt, m_w, c),
                         acc_init, n_back)
    o_slc = finish(value_tile(ss_ref, vst_ref, qi, qi, m_s, c_s))
    o_win = finish(value_tile(sw_ref, vwt_ref, qi - first, qi, m_w, c_w))

    gate = jax.nn.sigmoid(gate_ref[0, 0])
    for h in range(NSA_HPG):
        cols = slice(h * TQ, (h + 1) * TQ)
        o = (gate[3 * h:3 * h + 1, :] * o_cmp[:, cols] + gate[3 * h + 1:3 * h + 2, :] * o_slc[:, cols]
             + gate[3 * h + 2:3 * h + 3, :] * o_win[:, cols])
        o_ref[0, :, h * HEAD_DIM:(h + 1) * HEAD_DIM] = o.T.astype(o_ref.dtype)


def _nsa_attention(qn, kcmp, vcmp_t, ksn, vs_t, kwn, vw_t, gate_logits_t):
    B, S, _ = qn.shape
    G = NSA_KV_GROUPS
    TQ = ATTN_TQ
    n_cmp = S // CMP_STRIDE
    n_slc = S // SLC_BLOCK
    assert n_cmp == LANES and WINDOW % LANES == 0 and TQ == LANES
    cmp_start = np.arange(n_cmp) * CMP_STRIDE
    slc_start = np.arange(n_slc) * SLC_BLOCK
    overlap = np.clip(np.minimum(cmp_start[None, :] + CMP_BLOCK, slc_start[:, None] + SLC_BLOCK)
                      - np.maximum(cmp_start[None, :], slc_start[:, None]), 0, None) / CMP_STRIDE
    ovl = jnp.asarray(overlap, dtype=BF16)
    expand = jnp.asarray((np.arange(S)[:, None] // SLC_BLOCK) == np.arange(n_slc)[None, :], dtype=BF16)
    gq = NSA_HPG * HEAD_DIM
    q_spec = pl.BlockSpec((1, TQ, gq), lambda b, g, i: (b, i, g))
    k_spec = pl.BlockSpec((1, S, HEAD_DIM), lambda b, g, i: (b, 0, g))
    vt_spec = pl.BlockSpec((1, HEAD_DIM, S), lambda b, g, i: (b, g, 0))
    return pl.pallas_call(
        _nsa_attn_kernel,
        out_shape=jax.ShapeDtypeStruct((B, S, NSA_Q), BF16),
        grid=(B, G, S // TQ),
        in_specs=[q_spec,
                  pl.BlockSpec((1, 1, n_cmp, HEAD_DIM), lambda b, g, i: (b, g, 0, 0)),
                  pl.BlockSpec((1, 1, HEAD_DIM, n_cmp), lambda b, g, i: (b, g, 0, 0)),
                  k_spec, vt_spec, k_spec, vt_spec,
                  pl.BlockSpec((1, 1, 3 * NSA_HPG, TQ), lambda b, g, i: (b, g, 0, i)),
                  pl.BlockSpec(ovl.shape, lambda b, g, i: (0, 0)),
                  pl.BlockSpec(expand.shape, lambda b, g, i: (0, 0))],
        out_specs=q_spec,
        scratch_shapes=[pltpu.VMEM((S, TQ), F32),
                        pltpu.VMEM((S // LANES, LANES, NSA_HPG * TQ), F32),
                        pltpu.VMEM((WINDOW // LANES + 1, LANES, NSA_HPG * TQ), F32)],
        compiler_params=_cp("parallel", "parallel", "parallel"),
        name="nsa_attention",
    )(qn, kcmp, vcmp_t, ksn, vs_t, kwn, vw_t, gate_logits_t, ovl, expand)


def _gla_kernel(q_ref, k_ref, v0_ref, v1_ref, r0_ref, r1_ref, a_ref, wa_ref, ba_ref, ng_ref, o_ref, state_ref,
                *, a_off):
    C, DK, DV = GLA_CHUNK, GLA_DK, GLA_DV
    half = GLA_HEADS // 2
    v_refs, r_refs = (v0_ref, v1_ref), (r0_ref, r1_ref)

    @pl.when(pl.program_id(1) == 0)
    def _():
        state_ref[...] = jnp.zeros_like(state_ref)

    row = lax.broadcasted_iota(jnp.int32, (C, C), 0)
    colc = lax.broadcasted_iota(jnp.int32, (C, C), 1)
    causal = colc <= row
    tri = jnp.where(causal, 1.0, 0.0).astype(BF16)
    wa = wa_ref[...].astype(BF16)
    heads = range(GLA_HEADS)
    for c in range(q_ref.shape[1] // C):
        rows = slice(c * C, (c + 1) * C)
        a_low = a_ref[0, rows, a_off:a_off + GLA_RANK].astype(BF16)
        dk = [slice(h * DK, (h + 1) * DK) for h in heads]
        dv = [slice(h * DV, (h + 1) * DV) for h in heads]
        dvh = [slice((h % half) * DV, (h % half + 1) * DV) for h in heads]
        z = [_dot(a_low, wa[:, dk[h]]) + ba_ref[:, dk[h]] for h in heads]
        la = [(jnp.minimum(z[h], 0.0) - jnp.log1p(jnp.exp(-jnp.abs(z[h])))) / GLA_TAU for h in heads]
        hi = [la[h].astype(BF16) for h in heads]
        r1 = [la[h] - hi[h].astype(F32) for h in heads]
        mid = [r1[h].astype(BF16) for h in heads]
        lo = [(r1[h] - mid[h].astype(F32)).astype(BF16) for h in heads]
        bcum = [_dot(tri, hi[h]) + _dot(tri, mid[h]) + _dot(tri, lo[h]) for h in heads]
        blast = [bcum[h][C - 1:C, :] for h in heads]
        kh = [k_ref[0, rows, dk[h]] for h in heads]
        q_in = [(q_ref[0, rows, dk[h]] * (DK ** -0.5) * jnp.exp(bcum[h])).astype(BF16) for h in heads]
        k_in = [(kh[h] * jnp.exp(-bcum[h])).astype(BF16) for h in heads]
        k_out = [(kh[h] * jnp.exp(blast[h] - bcum[h])).astype(BF16) for h in heads]
        vb = [v_refs[h // half][0, rows, dvh[h]].astype(BF16) for h in heads]
        a_intra = [jnp.where(causal, _dot_nt(q_in[h], k_in[h]), 0.0).astype(BF16) for h in heads]
        state = [state_ref[h] for h in heads]
        o = [_dot(a_intra[h], vb[h]) + _dot_nt(q_in[h], state[h].astype(BF16)) for h in heads]
        upd = [_dot_tn(vb[h], k_out[h]) for h in heads]
        for h in heads:
            state_ref[h] = state[h] * jnp.exp(blast[h]) + upd[h]
            y = o[h] * lax.rsqrt(jnp.mean(o[h] * o[h], axis=-1, keepdims=True) + 1e-6) * ng_ref[...]
            o_ref[0, rows, dv[h]] = (y * _silu(r_refs[h // half][0, rows, dvh[h]])).astype(o_ref.dtype)


def _gla(proj, proj_small, a_off, w_a2, b_a, norm_g, q_col):
    B, S, _ = proj.shape
    H, DK, DV, TC = GLA_HEADS, GLA_DK, GLA_DV, GLA_TC
    hv = H * DV // 2
    qb = q_col // (H * DK)
    vb = (q_col + 2 * H * DK) // hv
    assert q_col % (H * DK) == 0 and (q_col + 2 * H * DK) % hv == 0
    return pl.pallas_call(
        functools.partial(_gla_kernel, a_off=a_off),
        out_shape=jax.ShapeDtypeStruct((B, S, H * DV), BF16),
        grid=(B, S // TC),
        in_specs=[pl.BlockSpec((1, TC, H * DK), lambda b, c: (b, c, qb)),
                  pl.BlockSpec((1, TC, H * DK), lambda b, c: (b, c, qb + 1)),
                  pl.BlockSpec((1, TC, hv), lambda b, c: (b, c, vb)),
                  pl.BlockSpec((1, TC, hv), lambda b, c: (b, c, vb + 1)),
                  pl.BlockSpec((1, TC, hv), lambda b, c: (b, c, vb + 2)),
                  pl.BlockSpec((1, TC, hv), lambda b, c: (b, c, vb + 3)),
                  pl.BlockSpec((1, TC, LANES), lambda b, c: (b, c, 0)),
                  pl.BlockSpec((GLA_RANK, H * DK), lambda b, c: (0, 0)),
                  pl.BlockSpec((1, H * DK), lambda b, c: (0, 0)),
                  pl.BlockSpec((1, DV), lambda b, c: (0, 0))],
        out_specs=pl.BlockSpec((1, TC, H * DV), lambda b, c: (b, c, 0)),
        scratch_shapes=[pltpu.VMEM((H, DV, DK), F32)],
        compiler_params=_cp("parallel", "arbitrary"),
        name="gla",
    )(proj, proj, proj, proj, proj, proj, proj_small, w_a2, b_a.reshape(1, H * DK), norm_g.reshape(1, DV))


def _conv_ln_kernel(u_ref, halo_ref, w_ref, b_ref, g_ref, beta_ref, o_ref, cat_ref, y_ref):
    TS, HALO, CH = CONV_TS, CONV_HALO, CONV_LANE_CHUNK
    D = u_ref.shape[2]
    n_chunks = D // CH
    first = HALO - (CONV_WIDTH - 1)
    ext = TS + SUBLANES

    @pl.when(pl.program_id(1) == 0)
    def _():
        cat_ref[0:HALO, :] = jnp.zeros((HALO, D), F32)

    @pl.when(pl.program_id(1) > 0)
    def _():
        cat_ref[0:HALO, :] = halo_ref[0]

    cat_ref[HALO:HALO + TS, :] = u_ref[0]
    cat_ref[HALO + TS:HALO + ext, :] = jnp.zeros((SUBLANES, D), F32)

    def conv_chunk(c, total):
        lanes = pl.ds(pl.multiple_of(c * CH, CH), CH)
        acc = jnp.zeros((TS, CH), F32) + b_ref[:, lanes]
        for r in range(SUBLANES):
            part = None
            for a in range((first + CONV_WIDTH - 1) // SUBLANES + 1):
                k = SUBLANES * a + r - first
                if 0 <= k < CONV_WIDTH:
                    term = cat_ref[pl.ds(SUBLANES * a, ext), lanes] * w_ref[pl.ds(k, 1), lanes]
                    part = term if part is None else part + term
            acc = acc + part[r:r + TS]
        y_ref[:, lanes] = acc
        return total + acc

    total = lax.fori_loop(0, n_chunks, conv_chunk, jnp.zeros((TS, CH), F32))
    mu = jnp.broadcast_to(jnp.sum(total, axis=-1, keepdims=True) / D, (TS, CH))

    def var_chunk(c, sq):
        d = y_ref[:, pl.ds(pl.multiple_of(c * CH, CH), CH)] - mu
        return sq + d * d

    sq = lax.fori_loop(0, n_chunks, var_chunk, jnp.zeros((TS, CH), F32))
    inv = jnp.broadcast_to(lax.rsqrt(jnp.sum(sq, axis=-1, keepdims=True) / D + 1e-5), (TS, CH))

    def out_chunk(c, carry):
        lanes = pl.ds(pl.multiple_of(c * CH, CH), CH)
        z = (y_ref[:, lanes] - mu) * inv * g_ref[:, lanes] + beta_ref[:, lanes]
        o_ref[0, :, lanes] = _silu(z).astype(o_ref.dtype)
        return carry

    lax.fori_loop(0, n_chunks, out_chunk, 0)


def _conv_ln_silu(u, w_dw, b_dw, ln_g, ln_b):
    B, S, D = u.shape
    TS, HALO = CONV_TS, CONV_HALO
    ratio = TS // HALO
    vec = pl.BlockSpec((1, D), lambda b, s: (0, 0))
    return pl.pallas_call(
        _conv_ln_kernel,
        out_shape=jax.ShapeDtypeStruct((B, S, D), BF16),
        grid=(B, S // TS),
        in_specs=[pl.BlockSpec((1, TS, D), lambda b, s: (b, s, 0)),
                  pl.BlockSpec((1, HALO, D), lambda b, s: (b, jnp.maximum(s * ratio - 1, 0), 0)),
                  pl.BlockSpec((CONV_WIDTH, D), lambda b, s: (0, 0)),
                  vec, vec, vec],
        out_specs=pl.BlockSpec((1, TS, D), lambda b, s: (b, s, 0)),
        scratch_shapes=[pltpu.VMEM((HALO + TS + SUBLANES, D), F32), pltpu.VMEM((TS, D), F32)],
        compiler_params=_cp("parallel", "parallel"),
        name="conv_ln_silu",
    )(u, u, w_dw, b_dw.reshape(1, D), ln_g.reshape(1, D), ln_b.reshape(1, D))


def _hybrid_attention(h, x2, gate, j, w_in, w_out, q_norm_g, k_norm_g, cmp_pos, cmp_w1, cmp_w2,
                      gla_w_a2, gla_b_a, gla_norm_g, B, S):
    D = h.shape[1]
    n_gate = NSA_HEADS * 3
    o_gl = NSA_Q + 6 * NSA_KV
    o_gq = o_gl + n_gate
    o_ga = o_gq + 2 * GLA_HEADS * GLA_DK + GLA_HEADS * GLA_DV
    o_gr = o_ga + GLA_RANK
    w_main, w_small = _regroup_w_in(w_in, j, ((0, o_gl), (o_gq, o_ga), (o_gr, w_in.shape[2])),
                                    ((o_gl, o_gq), (o_ga, o_gr)), 128)
    proj = _matmul(h, w_main, F32, 1024, 512).reshape(B, S, -1)
    proj_small = _matmul(h, w_small, F32, 1024, LANES).reshape(B, S, LANES)

    t = jnp.arange(S, dtype=jnp.int32)
    cmp_end = jnp.arange(S // CMP_STRIDE, dtype=jnp.int32) * CMP_STRIDE + (CMP_BLOCK - 1)
    qn, ksn, vs_t, kwn, vw_t = _nsa_prep(proj, _rope_tables(t), q_norm_g, k_norm_g)
    kcmp, vcmp_t = _nsa_compress(proj, cmp_pos, cmp_w1, cmp_w2, k_norm_g, _rope_tables(cmp_end))
    gate_logits_t = proj_small[:, :, :n_gate].reshape(B, S, NSA_KV_GROUPS, 3 * NSA_HPG).transpose(0, 2, 3, 1)
    o_nsa = _nsa_attention(qn, kcmp, vcmp_t, ksn, vs_t, kwn, vw_t, gate_logits_t)
    o_gla = _gla(proj, proj_small, n_gate, gla_w_a2, gla_b_a, gla_norm_g, o_gl)
    xs = (o_nsa.reshape(B * S, -1), o_gla.reshape(B * S, -1))
    return _matmul_residual(xs, w_out, j, jnp.zeros((1, D), F32), x2, gate, S, 1024, 512)


def _conformer(h, x2, gate, j, w_pw1, b_pw1, w_dw, b_dw, ln_g, ln_b, w_pw2, b_pw2, B, S):
    D = h.shape[1]
    tn = 256
    b1 = b_pw1.reshape(1, 2 * D)
    u = _matmul_glu(h, w_pw1, w_pw1, j, b1, b1, D // tn, False, F32, 1024, tn)
    v = _conv_ln_silu(u.reshape(B, S, D), w_dw, b_dw, ln_g, ln_b).reshape(B * S, D)
    return _matmul_residual((v,), w_pw2, j, b_pw2.reshape(1, D), x2, gate, S, 1024, 512)


def _swiglu(h, x2, gate, layer, w_gate, w_up, w_down, S):
    D = h.shape[1]
    F = w_gate.shape[2]
    zeros = jnp.zeros((1, F), F32)
    act = _matmul_glu(h, w_gate, w_up, layer, zeros, zeros, 0, True, BF16, 1024, 256)
    return _matmul_residual((act,), _cast_bf16(w_down, layer, 256), None, jnp.zeros((1, D), F32), x2, gate, S,
                            1024, 256)


def kernel(x, c, w_mod, b_mod, ada_table, norm_mix_g, norm_ffn_g, w_in, w_out, q_norm_g, k_norm_g, cmp_pos, cmp_w1, cmp_w2, gla_w_a2, gla_b_a, gla_norm_g, cv_w_pw1, cv_b_pw1, cv_w_dw, cv_b_dw, cv_ln_g, cv_ln_b, cv_w_pw2, cv_b_pw2, ffn_w_gate, ffn_w_up, ffn_w_down):
    B, S, D = x.shape
    depth = ada_table.shape[0]
    mod = _ada_mod(c, w_mod, b_mod).reshape(B, N_MOD, D)
    for layer in range(depth):
        m = mod + ada_table[layer]
        sh_a, sc_a, g_a, sh_f, sc_f, g_f = [m[:, i, :] for i in range(N_MOD)]
        h = _norm_mod(x, norm_mix_g[layer], sc_a, sh_a).reshape(B * S, D)
        x2 = x.reshape(B * S, D)
        g_a3 = g_a.reshape(B, 1, D)
        j = layer // 2
        if layer % 2 == 0:
            x2 = _hybrid_attention(h, x2, g_a3, j, w_in, w_out, q_norm_g[j], k_norm_g[j], cmp_pos[j],
                                   cmp_w1[j], cmp_w2[j], gla_w_a2[j], gla_b_a[j], gla_norm_g[j], B, S)
        else:
            x2 = _conformer(h, x2, g_a3, j, cv_w_pw1, cv_b_pw1[j], cv_w_dw[j], cv_b_dw[j], cv_ln_g[j],
                            cv_ln_b[j], cv_w_pw2, cv_b_pw2[j], B, S)
        x = x2.reshape(B, S, D)
        h = _norm_mod(x, norm_ffn_g[layer], sc_f, sh_f).reshape(B * S, D)
        x2 = _swiglu(h, x2, g_f.reshape(B, 1, D), layer, ffn_w_gate, ffn_w_up, ffn_w_down, S)
        x = x2.reshape(B, S, D)
    return x
```

```python
import functools

import numpy as np
import jax
import jax.numpy as jnp
from jax import lax
from jax.experimental import pallas as pl
from jax.experimental.pallas import tpu as pltpu

F32 = jnp.float32
BF16 = jnp.bfloat16

HEAD_DIM = 128
NSA_HEADS = 16
NSA_KV_GROUPS = 4
NSA_HPG = NSA_HEADS // NSA_KV_GROUPS
CMP_BLOCK = 32
CMP_STRIDE = 16
SLC_BLOCK = 64
SLC_TOPK = 16
WINDOW = 512
ROPE_THETA = 500000.0
ROPE_DIM = HEAD_DIM // 4
GLA_HEADS = 4
GLA_DK = 256
GLA_DV = 512
GLA_RANK = 16
GLA_TAU = 16.0
GLA_CHUNK = 64
CONV_WIDTH = 31
N_MOD = 6
NSA_Q = NSA_HEADS * HEAD_DIM
NSA_KV = NSA_KV_GROUPS * HEAD_DIM

VMEM_LIMIT_BYTES = 56 * 1024 * 1024
LANES = 128
SUBLANES = 8
NEG_BIG = -1e30
M_INIT = -1e29

ATTN_TQ = 128
ATTN_UNROLL = 4
CONV_TS = 128
CONV_HALO = 32
CONV_LANE_CHUNK = 128
GLA_TC = 128


def _cp(*sem):
    return pltpu.CompilerParams(dimension_semantics=sem, vmem_limit_bytes=VMEM_LIMIT_BYTES)


def _dot(a, b):
    return jnp.dot(a, b, preferred_element_type=F32)


def _dot_nt(a, b):
    return lax.dot_general(a, b, (((1,), (1,)), ((), ())), preferred_element_type=F32)


def _dot_tn(a, b):
    return lax.dot_general(a, b, (((0,), (0,)), ((), ())), preferred_element_type=F32)


def _silu(x):
    return x * jax.nn.sigmoid(x)


def _cast_weights_once(w_refs, wbf_refs):
    @pl.when(pl.program_id(1) == 0)
    def _():
        for w_ref, wbf_ref in zip(w_refs, wbf_refs):
            wbf_ref[...] = w_ref[...].astype(BF16)


def _mm_kernel(x_ref, w_ref, o_ref):
    o_ref[...] = _dot(x_ref[...], w_ref[...]).astype(o_ref.dtype)


def _matmul(x, w, out_dtype, tm, tn):
    M, K = x.shape
    N = w.shape[1]
    return pl.pallas_call(
        _mm_kernel,
        out_shape=jax.ShapeDtypeStruct((M, N), out_dtype),
        grid=(M // tm, N // tn),
        in_specs=[pl.BlockSpec((tm, K), lambda i, j: (i, 0)),
                  pl.BlockSpec((K, tn), lambda i, j: (0, j))],
        out_specs=pl.BlockSpec((tm, tn), lambda i, j: (i, j)),
        compiler_params=_cp("parallel", "parallel"),
        name="matmul",
    )(x, w)


def _mm_swiglu_kernel(x_ref, wg_ref, wu_ref, o_ref, wgbf_ref, wubf_ref):
    _cast_weights_once((wg_ref, wu_ref), (wgbf_ref, wubf_ref))
    x = x_ref[...]
    o_ref[...] = (_silu(_dot(x, wgbf_ref[...])) * _dot(x, wubf_ref[...])).astype(o_ref.dtype)


def _matmul_swiglu(x, w_gate, w_up, layer, out_dtype, tm, tn):
    M, K = x.shape
    N = w_gate.shape[2]
    w_spec = pl.BlockSpec((None, K, tn), lambda j, i: (layer, 0, j))
    return pl.pallas_call(
        _mm_swiglu_kernel,
        out_shape=jax.ShapeDtypeStruct((M, N), out_dtype),
        grid=(N // tn, M // tm),
        in_specs=[pl.BlockSpec((tm, K), lambda j, i: (i, 0)), w_spec, w_spec],
        out_specs=pl.BlockSpec((tm, tn), lambda j, i: (i, j)),
        scratch_shapes=[pltpu.VMEM((K, tn), BF16), pltpu.VMEM((K, tn), BF16)],
        compiler_params=_cp("arbitrary", "arbitrary"),
        name="matmul_swiglu",
    )(x, w_gate, w_up)


def _mm_glu_pair_kernel(x_ref, w_ref, b_ref, o_ref):
    tn = o_ref.shape[1]
    x = x_ref[...]
    a = _dot(x, w_ref[:, :tn]) + b_ref[:, :tn]
    b = _dot(x, w_ref[:, tn:]) + b_ref[:, tn:]
    o_ref[...] = (a * jax.nn.sigmoid(b)).astype(o_ref.dtype)


def _matmul_glu_pair(x, w_pair, b_pair, out_dtype, tm, tn):
    M, K = x.shape
    N = w_pair.shape[1] // 2
    return pl.pallas_call(
        _mm_glu_pair_kernel,
        out_shape=jax.ShapeDtypeStruct((M, N), out_dtype),
        grid=(M // tm, N // tn),
        in_specs=[pl.BlockSpec((tm, K), lambda i, j: (i, 0)),
                  pl.BlockSpec((K, 2 * tn), lambda i, j: (0, j)),
                  pl.BlockSpec((1, 2 * tn), lambda i, j: (0, j))],
        out_specs=pl.BlockSpec((tm, tn), lambda i, j: (i, j)),
        compiler_params=_cp("parallel", "parallel"),
        name="matmul_glu_pair",
    )(x, w_pair, b_pair)


def _pair_cast_kernel(w_ref, o_ref, *, tn):
    half = w_ref.shape[1] // 2
    for j in range(half // tn):
        o_ref[:, 2 * j * tn:(2 * j + 1) * tn] = w_ref[:, j * tn:(j + 1) * tn].astype(BF16)
        o_ref[:, (2 * j + 1) * tn:(2 * j + 2) * tn] = w_ref[:, half + j * tn:half + (j + 1) * tn].astype(BF16)


def _pair_cast(w, layer, tn, rows):
    _, K, N2 = w.shape
    return pl.pallas_call(
        functools.partial(_pair_cast_kernel, tn=tn),
        out_shape=jax.ShapeDtypeStruct((K, N2), BF16),
        grid=(K // rows,),
        in_specs=[pl.BlockSpec((None, rows, N2), lambda i: (layer, i, 0))],
        out_specs=pl.BlockSpec((rows, N2), lambda i: (i, 0)),
        compiler_params=_cp("parallel"),
        name="pair_cast",
    )(w)


def _mm_res_kernel(*refs, n_x, cast):
    x_refs = refs[:n_x]
    if cast:
        w_ref, b_ref, res_ref, g_ref, o_ref, wbf_ref = refs[n_x:]
        _cast_weights_once((w_ref,), (wbf_ref,))
    else:
        wbf_ref, b_ref, res_ref, g_ref, o_ref = refs[n_x:]
    y = b_ref[...]
    off = 0
    for x_ref in x_refs:
        k = x_ref.shape[1]
        y = y + _dot(x_ref[...], wbf_ref[off:off + k, :])
        off += k
    o_ref[...] = res_ref[...] + g_ref[0] * y


def _matmul_residual(xs, w, layer, bias, res, gate, rows_per_batch, tm, tn):
    M = xs[0].shape[0]
    K, N = w.shape[-2:]
    bpt = rows_per_batch // tm
    cast = w.ndim == 3
    w_block = (None, K, tn) if cast else (K, tn)
    w_index = (lambda i, j: (layer, 0, j)) if cast else (lambda i, j: (0, j))
    if cast:
        grid = (N // tn, M // tm)
        ij = lambda f: (lambda j, i: f(i, j))
        x_mode, w_mode = None, pl.Buffered(1)
        scratch = [pltpu.VMEM((K, tn), BF16)]
    else:
        grid = (M // tm, N // tn)
        ij = lambda f: f
        x_mode, w_mode = pl.Buffered(1), None
        scratch = []
    return pl.pallas_call(
        functools.partial(_mm_res_kernel, n_x=len(xs), cast=cast),
        out_shape=jax.ShapeDtypeStruct((M, N), F32),
        grid=grid,
        in_specs=[pl.BlockSpec((tm, x.shape[1]), ij(lambda i, j: (i, 0)), pipeline_mode=x_mode) for x in xs] + [
                  pl.BlockSpec(w_block, ij(w_index), pipeline_mode=w_mode),
                  pl.BlockSpec((1, tn), ij(lambda i, j: (0, j))),
                  pl.BlockSpec((tm, tn), ij(lambda i, j: (i, j))),
                  pl.BlockSpec((1, 1, tn), ij(lambda i, j: (i // bpt, 0, j)))],
        out_specs=pl.BlockSpec((tm, tn), ij(lambda i, j: (i, j))),
        scratch_shapes=scratch,
        compiler_params=_cp("arbitrary", "arbitrary"),
        name="matmul_residual",
    )(*xs, w, bias, res, gate)


def _cast_kernel(w_ref, o_ref):
    o_ref[...] = w_ref[...].astype(o_ref.dtype)


def _cast_bf16(w, layer, rows):
    _, K, N = w.shape
    return pl.pallas_call(
        _cast_kernel,
        out_shape=jax.ShapeDtypeStruct((K, N), BF16),
        grid=(K // rows,),
        in_specs=[pl.BlockSpec((None, rows, N), lambda i: (layer, i, 0))],
        out_specs=pl.BlockSpec((rows, N), lambda i: (i, 0)),
        compiler_params=_cp("parallel"),
        name="cast_bf16",
    )(w)


def _regroup_kernel(w_ref, main_ref, small_ref, *, segments, small_segments):
    off = 0
    for lo, hi in segments:
        main_ref[:, off:off + hi - lo] = w_ref[:, lo:hi].astype(BF16)
        off += hi - lo
    small_ref[...] = jnp.zeros(small_ref.shape, BF16)
    off = 0
    for lo, hi in small_segments:
        small_ref[:, off:off + hi - lo] = w_ref[:, lo:hi].astype(BF16)
        off += hi - lo


def _regroup_w_in(w_in, layer, segments, small_segments, rows):
    _, K, N = w_in.shape
    n_main = sum(hi - lo for lo, hi in segments)
    return pl.pallas_call(
        functools.partial(_regroup_kernel, segments=segments, small_segments=small_segments),
        out_shape=(jax.ShapeDtypeStruct((K, n_main), BF16), jax.ShapeDtypeStruct((K, LANES), BF16)),
        grid=(K // rows,),
        in_specs=[pl.BlockSpec((None, rows, N), lambda i: (layer, i, 0))],
        out_specs=(pl.BlockSpec((rows, n_main), lambda i: (i, 0)), pl.BlockSpec((rows, LANES), lambda i: (i, 0))),
        compiler_params=_cp("parallel"),
        name="regroup_w_in",
    )(w_in)


def _mod_kernel(c_ref, w_ref, b_ref, o_ref):
    a = _silu(c_ref[...]).astype(BF16)
    o_ref[...] = _dot(a, w_ref[...].astype(BF16)) + b_ref[...]


def _ada_mod(c, w_mod, b_mod):
    B, D = c.shape
    N = w_mod.shape[1]
    tn = 512
    return pl.pallas_call(
        _mod_kernel,
        out_shape=jax.ShapeDtypeStruct((B, N), F32),
        grid=(N // tn,),
        in_specs=[pl.BlockSpec((B, D), lambda j: (0, 0)),
                  pl.BlockSpec((D, tn), lambda j: (0, j)),
                  pl.BlockSpec((1, tn), lambda j: (0, j))],
        out_specs=pl.BlockSpec((B, tn), lambda j: (0, j)),
        compiler_params=_cp("parallel"),
        name="ada_mod",
    )(c, w_mod, b_mod.reshape(1, N))


def _norm_mod_kernel(x_ref, g_ref, sc_ref, sh_ref, o_ref, gm_ref):
    rows_per_trip = 2 * SUBLANES
    gm_ref[...] = g_ref[...] * (1.0 + sc_ref[0])

    def trip(i, carry):
        rows = pl.ds(pl.multiple_of(i * rows_per_trip, rows_per_trip), rows_per_trip)
        x = x_ref[0, rows, :]
        r = lax.rsqrt(jnp.mean(x * x, axis=-1, keepdims=True) + 1e-6)
        o_ref[0, rows, :] = (x * r * gm_ref[...] + sh_ref[0]).astype(o_ref.dtype)
        return carry

    lax.fori_loop(0, x_ref.shape[1] // rows_per_trip, trip, 0, unroll=4)


def _norm_mod(x, g, sc, sh):
    B, S, D = x.shape
    ts = 512
    return pl.pallas_call(
        _norm_mod_kernel,
        out_shape=jax.ShapeDtypeStruct((B, S, D), BF16),
        grid=(B, S // ts),
        in_specs=[pl.BlockSpec((1, ts, D), lambda b, s: (b, s, 0)),
                  pl.BlockSpec((1, D), lambda b, s: (0, 0)),
                  pl.BlockSpec((1, 1, D), lambda b, s: (b, 0, 0)),
                  pl.BlockSpec((1, 1, D), lambda b, s: (b, 0, 0))],
        out_specs=pl.BlockSpec((1, ts, D), lambda b, s: (b, s, 0)),
        scratch_shapes=[pltpu.VMEM((1, D), F32)],
        compiler_params=_cp("parallel", "parallel"),
        name="norm_mod",
    )(x, g.reshape(1, D), sc.reshape(B, 1, D), sh.reshape(B, 1, D))


def _rope_tables(pos):
    half = ROPE_DIM // 2
    inv_freq = ROPE_THETA ** (-jnp.arange(half, dtype=F32) / half)
    ang = pos.astype(F32)[:, None] * inv_freq[None, :]
    cos, sin = jnp.cos(ang), jnp.sin(ang)
    n = pos.shape[0]
    rest = HEAD_DIM - ROPE_DIM
    c = jnp.concatenate([cos, cos, jnp.ones((n, rest), F32)], axis=-1)
    s_lo = jnp.concatenate([-sin, jnp.zeros((n, HEAD_DIM - half), F32)], axis=-1)
    s_hi = jnp.concatenate([jnp.zeros((n, half), F32), sin, jnp.zeros((n, rest), F32)], axis=-1)
    return c, s_lo, s_hi


def _norm_rope(x, g, c, s_lo, s_hi):
    y = x * lax.rsqrt(jnp.mean(x * x, axis=-1, keepdims=True) + 1e-6) * g
    half = ROPE_DIM // 2
    return y * c + pltpu.roll(y, HEAD_DIM - half, 1) * s_lo + pltpu.roll(y, half, 1) * s_hi


def _nsa_prep_kernel(q_ref, ks_ref, vs_ref, kw_ref, vw_ref, c_ref, slo_ref, shi_ref, qg_ref, kg_ref,
                     qo_ref, kso_ref, vso_ref, kwo_ref, vwo_ref):
    c, s_lo, s_hi = c_ref[...], slo_ref[...], shi_ref[...]
    scale = HEAD_DIM ** -0.5
    for h in range(NSA_HEADS):
        sl = slice(h * HEAD_DIM, (h + 1) * HEAD_DIM)
        qo_ref[0, :, sl] = (_norm_rope(q_ref[0, :, sl], qg_ref[...], c, s_lo, s_hi) * scale).astype(BF16)
    for g in range(NSA_KV_GROUPS):
        sl = slice(g * HEAD_DIM, (g + 1) * HEAD_DIM)
        kso_ref[0, :, sl] = _norm_rope(ks_ref[0, :, sl], kg_ref[1:2, :], c, s_lo, s_hi).astype(BF16)
        kwo_ref[0, :, sl] = _norm_rope(kw_ref[0, :, sl], kg_ref[2:3, :], c, s_lo, s_hi).astype(BF16)
    vso_ref[0] = vs_ref[0].T.astype(BF16)
    vwo_ref[0] = vw_ref[0].T.astype(BF16)


def _nsa_prep(proj, tables, q_norm_g, k_norm_g):
    B, S, _ = proj.shape
    ts = 256
    kvb = NSA_Q // NSA_KV

    def kv_spec(n):
        return pl.BlockSpec((1, ts, NSA_KV), lambda b, s: (b, s, kvb + n))

    tab = pl.BlockSpec((ts, HEAD_DIM), lambda b, s: (s, 0))
    out_k = pl.BlockSpec((1, ts, NSA_KV), lambda b, s: (b, s, 0))
    out_vt = pl.BlockSpec((1, NSA_KV, ts), lambda b, s: (b, 0, s))
    k_shape = jax.ShapeDtypeStruct((B, S, NSA_KV), BF16)
    vt_shape = jax.ShapeDtypeStruct((B, NSA_KV, S), BF16)
    return pl.pallas_call(
        _nsa_prep_kernel,
        out_shape=(jax.ShapeDtypeStruct((B, S, NSA_Q), BF16), k_shape, vt_shape, k_shape, vt_shape),
        grid=(B, S // ts),
        in_specs=[pl.BlockSpec((1, ts, NSA_Q), lambda b, s: (b, s, 0)),
                  kv_spec(2), kv_spec(3), kv_spec(4), kv_spec(5), tab, tab, tab,
                  pl.BlockSpec((1, HEAD_DIM), lambda b, s: (0, 0)),
                  pl.BlockSpec((3, HEAD_DIM), lambda b, s: (0, 0))],
        out_specs=(pl.BlockSpec((1, ts, NSA_Q), lambda b, s: (b, s, 0)), out_k, out_vt, out_k, out_vt),
        compiler_params=_cp("parallel", "parallel"),
        name="nsa_prep",
    )(proj, proj, proj, proj, proj, *tables, q_norm_g.reshape(1, HEAD_DIM), k_norm_g)


def _compress_kernel(kc_ref, vc_ref, pos_ref, w1_ref, w2_ref, kg_ref, c_ref, slo_ref, shi_ref,
                     ko_ref, vo_ref):
    half_blk = CMP_BLOCK // 2
    n_seg = kc_ref.shape[1] // CMP_STRIDE

    def compress(tok_ref, j):
        u = jnp.zeros((n_seg, w1_ref.shape[2]), F32)
        v = jnp.zeros((n_seg, w1_ref.shape[2]), F32)
        for l in range(half_blk):
            x = tok_ref[0, pl.ds(l, n_seg, stride=CMP_STRIDE), :]
            xa = (x + pos_ref[j, l:l + 1, :]).astype(BF16)
            xb = (x + pos_ref[j, half_blk + l:half_blk + l + 1, :]).astype(BF16)
            u = u + _dot(xa, w1_ref[j, l * HEAD_DIM:(l + 1) * HEAD_DIM, :])
            v = v + _dot(xb, w1_ref[j, (half_blk + l) * HEAD_DIM:(half_blk + l + 1) * HEAD_DIM, :])
        h = u + pltpu.roll(v, n_seg - 1, 0)
        return _dot(jax.nn.gelu(h).astype(BF16), w2_ref[j])

    k = compress(kc_ref, 0)
    ko_ref[0, 0] = _norm_rope(k, kg_ref[0:1, :], c_ref[...], slo_ref[...], shi_ref[...]).astype(BF16)
    vo_ref[0, 0] = compress(vc_ref, 1).T.astype(BF16)


def _nsa_compress(proj, cmp_pos, cmp_w1, cmp_w2, k_norm_g, cmp_tables):
    B, S, _ = proj.shape
    G = NSA_KV_GROUPS
    n_seg = S // CMP_STRIDE
    kcb = NSA_Q // HEAD_DIM
    full2 = lambda b, g: (0, 0)
    full3 = lambda b, g: (0, 0, 0)
    return pl.pallas_call(
        _compress_kernel,
        out_shape=(jax.ShapeDtypeStruct((B, G, n_seg, HEAD_DIM), BF16),
                   jax.ShapeDtypeStruct((B, G, HEAD_DIM, n_seg), BF16)),
        grid=(B, G),
        in_specs=[pl.BlockSpec((1, S, HEAD_DIM), lambda b, g: (b, 0, kcb + g)),
                  pl.BlockSpec((1, S, HEAD_DIM), lambda b, g: (b, 0, kcb + G + g)),
                  pl.BlockSpec(cmp_pos.shape, full3),
                  pl.BlockSpec(cmp_w1.shape, full3),
                  pl.BlockSpec(cmp_w2.shape, full3),
                  pl.BlockSpec((3, HEAD_DIM), full2),
                  pl.BlockSpec((n_seg, HEAD_DIM), full2),
                  pl.BlockSpec((n_seg, HEAD_DIM), full2),
                  pl.BlockSpec((n_seg, HEAD_DIM), full2)],
        out_specs=(pl.BlockSpec((1, 1, n_seg, HEAD_DIM), lambda b, g: (b, g, 0, 0)),
                   pl.BlockSpec((1, 1, HEAD_DIM, n_seg), lambda b, g: (b, g, 0, 0))),
        compiler_params=_cp("parallel", "parallel"),
        name="nsa_compress",
    )(proj, proj, cmp_pos, cmp_w1.astype(BF16), cmp_w2.astype(BF16), k_norm_g, *cmp_tables)


def _fold_rows(x, op):
    return op(x.reshape(x.shape[0] // SUBLANES, SUBLANES, x.shape[1]), axis=0)


def _nsa_attn_kernel(q_ref, kc_ref, vct_ref, ks_ref, vst_ref, kw_ref, vwt_ref, gate_ref, ovl_ref, exp_ref,
                     o_ref, selm_ref, ss_ref, sw_ref):
    TQ, TK = ATTN_TQ, LANES
    R = NSA_HPG * TQ
    qi = pl.program_id(2)
    t0 = qi * TQ
    q = jnp.concatenate([q_ref[0, :, h * HEAD_DIM:(h + 1) * HEAD_DIM] for h in range(NSA_HPG)], axis=0)
    tq = t0 + (lax.broadcasted_iota(jnp.int32, (TK, R), 1) & (TQ - 1))
    key = lax.broadcasted_iota(jnp.int32, (TK, R), 0)

    def score_tile(k_ref, scr_ref, slot, kt, m8, use_sel=False, causal=False, window=False, valid=None):
        off = pl.multiple_of(kt * TK, TK)
        s = _dot_nt(k_ref[0, pl.ds(off, TK), :], q)
        mask = None
        if use_sel:
            sm = selm_ref[pl.ds(off, TK), :]
            mask = jnp.concatenate([sm] * NSA_HPG, axis=1) > 0.5
        if causal:
            c = (off + key) <= tq
            mask = c if mask is None else mask & c
        if window:
            w = (tq - (off + key)) < WINDOW
            mask = w if mask is None else mask & w
        if valid is not None:
            mask = valid if mask is None else mask & valid
        if mask is not None:
            s = jnp.where(mask, s, NEG_BIG)
        scr_ref[slot] = s
        return jnp.maximum(m8, _fold_rows(s, jnp.max))

    def value_tile(scr_ref, vt_ref, slot, kt, m, carry):
        l8, acc = carry
        off = pl.multiple_of(kt * TK, TK)
        p = jnp.exp(scr_ref[slot] - m)
        return l8 + _fold_rows(p, jnp.sum), acc + _dot(vt_ref[0, :, pl.ds(off, TK)], p.astype(BF16))

    def finish(carry):
        l8, acc = carry
        l = jnp.sum(l8, axis=0, keepdims=True)
        return acc / jnp.where(l > 0.0, l, 1.0)

    m_init = jnp.full((SUBLANES, R), M_INIT, F32)
    acc_init = (jnp.zeros((SUBLANES, R), F32), jnp.zeros((HEAD_DIM, R), F32))

    n_back = WINDOW // TK
    win_tiles = [(jnp.maximum(qi - n_back + u, 0), qi - n_back + u >= 0) for u in range(n_back)]
    m8_w = m_init
    for u, (kt, valid) in enumerate(win_tiles):
        m8_w = score_tile(kw_ref, sw_ref, u, kt, m8_w, window=(u == 0), valid=valid)
    m8_w = score_tile(kw_ref, sw_ref, n_back, qi, m8_w, causal=True)
    m_w = jnp.max(m8_w, axis=0, keepdims=True)

    s = _dot_nt(kc_ref[0, 0], q)
    mask = (key * CMP_STRIDE + (CMP_BLOCK - 1)) <= tq
    s = jnp.where(mask, s, NEG_BIG)
    p = jnp.where(mask, jnp.exp(s - jnp.max(s, axis=0, keepdims=True)), 0.0)
    l = jnp.sum(p, axis=0, keepdims=True)
    pb = (p / jnp.where(l > 0.0, l, 1.0)).astype(BF16)
    o_cmp = _dot(vct_ref[0, 0], pb)

    c_w = acc_init
    for u, (kt, _) in enumerate(win_tiles):
        c_w = value_tile(sw_ref, vwt_ref, u, kt, m_w, c_w)
    o_win = finish(value_tile(sw_ref, vwt_ref, n_back, qi, m_w, c_w))

    n_slc = ovl_ref.shape[0]
    ranked = t0 + TQ > SLC_TOPK * SLC_BLOCK
    last_group = qi // ATTN_UNROLL
    group_keys = ATTN_UNROLL * TK

    @pl.when(ranked)
    def _():
        imp_heads = _dot(ovl_ref[...], pb)
        imp = imp_heads[:, 0:TQ]
        for h in range(1, NSA_HPG):
            imp = imp + imp_heads[:, h * TQ:(h + 1) * TQ]
        t = t0 + lax.broadcasted_iota(jnp.int32, (n_slc, TQ), 1)
        blk = lax.broadcasted_iota(jnp.int32, (n_slc, TQ), 0)
        cur = t // SLC_BLOCK
        forced = (blk == 0) | (blk == cur) | (blk == cur - 1)
        valid = blk * SLC_BLOCK <= t
        val = jnp.where(forced, jnp.inf, jnp.where(valid, imp, -jnp.inf))
        rank = jnp.zeros((n_slc, TQ), F32)
        for i in range(n_slc):
            vi = val[i:i + 1, :]
            ahead = (vi > val) | ((vi == val) & (blk > i))
            rank = rank + jnp.where(ahead, 1.0, 0.0)
        sel = jnp.where((rank < float(SLC_TOPK)) & (val > -jnp.inf), 1.0, 0.0).astype(BF16)
        selm_ref[...] = _dot(exp_ref[...], sel)

    @pl.when(jnp.logical_not(ranked))
    def _():
        rows = pl.ds(pl.multiple_of(last_group * group_keys, group_keys), group_keys)
        selm_ref[rows, :] = jnp.ones((group_keys, TQ), F32)

    def score_group(a, m8, use_sel, causal):
        for u in range(ATTN_UNROLL):
            kt = a * ATTN_UNROLL + u
            m8 = score_tile(ks_ref, ss_ref, kt, kt, m8, use_sel=use_sel, causal=causal)
        return m8

    n_masked = jnp.where(ranked, last_group, 0)
    m8 = lax.fori_loop(0, n_masked, lambda a, m: score_group(a, m, True, False), m_init)
    m8 = lax.fori_loop(0, last_group - n_masked, lambda a, m: score_group(a, m, False, False), m8)
    m8_s = score_group(last_group, m8, True, True)
    m_s = jnp.max(m8_s, axis=0, keepdims=True)

    def value_group(a, carry):
        for u in range(ATTN_UNROLL):
            kt = a * ATTN_UNROLL + u
            carry = value_tile(ss_ref, vst_ref, kt, kt, m_s, carry)
        return carry

    o_slc = finish(lax.fori_loop(0, last_group + 1, value_group, acc_init))

    gate = jax.nn.sigmoid(gate_ref[0, 0])
    for h in range(NSA_HPG):
        cols = slice(h * TQ, (h + 1) * TQ)
        o = (gate[3 * h:3 * h + 1, :] * o_cmp[:, cols] + gate[3 * h + 1:3 * h + 2, :] * o_slc[:, cols]
             + gate[3 * h + 2:3 * h + 3, :] * o_win[:, cols])
        o_ref[0, :, h * HEAD_DIM:(h + 1) * HEAD_DIM] = o.T.astype(o_ref.dtype)


def _nsa_attention(qn, kcmp, vcmp_t, ksn, vs_t, kwn, vw_t, gate_logits_t):
    B, S, _ = qn.shape
    G = NSA_KV_GROUPS
    TQ = ATTN_TQ
    n_cmp = S // CMP_STRIDE
    n_slc = S // SLC_BLOCK
    assert n_cmp == LANES and WINDOW % LANES == 0 and TQ == LANES
    cmp_start = np.arange(n_cmp) * CMP_STRIDE
    slc_start = np.arange(n_slc) * SLC_BLOCK
    overlap = np.clip(np.minimum(cmp_start[None, :] + CMP_BLOCK, slc_start[:, None] + SLC_BLOCK)
                      - np.maximum(cmp_start[None, :], slc_start[:, None]), 0, None) / CMP_STRIDE
    ovl = jnp.asarray(overlap, dtype=BF16)
    expand = jnp.asarray((np.arange(S)[:, None] // SLC_BLOCK) == np.arange(n_slc)[None, :], dtype=BF16)
    gq = NSA_HPG * HEAD_DIM
    q_spec = pl.BlockSpec((1, TQ, gq), lambda b, g, i: (b, i, g))
    k_spec = pl.BlockSpec((1, S, HEAD_DIM), lambda b, g, i: (b, 0, g))
    vt_spec = pl.BlockSpec((1, HEAD_DIM, S), lambda b, g, i: (b, g, 0))
    return pl.pallas_call(
        _nsa_attn_kernel,
        out_shape=jax.ShapeDtypeStruct((B, S, NSA_Q), BF16),
        grid=(B, G, S // TQ),
        in_specs=[q_spec,
                  pl.BlockSpec((1, 1, n_cmp, HEAD_DIM), lambda b, g, i: (b, g, 0, 0)),
                  pl.BlockSpec((1, 1, HEAD_DIM, n_cmp), lambda b, g, i: (b, g, 0, 0)),
                  k_spec, vt_spec, k_spec, vt_spec,
                  pl.BlockSpec((1, 1, 3 * NSA_HPG, TQ), lambda b, g, i: (b, g, 0, i)),
                  pl.BlockSpec(ovl.shape, lambda b, g, i: (0, 0)),
                  pl.BlockSpec(expand.shape, lambda b, g, i: (0, 0))],
        out_specs=q_spec,
        scratch_shapes=[pltpu.VMEM((S, TQ), F32),
                        pltpu.VMEM((S // LANES, LANES, NSA_HPG * TQ), F32),
                        pltpu.VMEM((WINDOW // LANES + 1, LANES, NSA_HPG * TQ), F32)],
        compiler_params=_cp("parallel", "parallel", "parallel"),
        name="nsa_attention",
    )(qn, kcmp, vcmp_t, ksn, vs_t, kwn, vw_t, gate_logits_t, ovl, expand)


def _gla_kernel(q_ref, k_ref, v0_ref, v1_ref, r0_ref, r1_ref, a_ref, wa_ref, ba_ref, ng_ref, o_ref, state_ref,
                *, a_off):
    C, DK, DV = GLA_CHUNK, GLA_DK, GLA_DV
    half = GLA_HEADS // 2
    v_refs, r_refs = (v0_ref, v1_ref), (r0_ref, r1_ref)

    @pl.when(pl.program_id(1) == 0)
    def _():
        state_ref[...] = jnp.zeros_like(state_ref)

    row = lax.broadcasted_iota(jnp.int32, (C, C), 0)
    colc = lax.broadcasted_iota(jnp.int32, (C, C), 1)
    causal = colc <= row
    tri = jnp.where(causal, 1.0, 0.0).astype(BF16)
    wa = wa_ref[...].astype(BF16)
    heads = range(GLA_HEADS)
    for c in range(q_ref.shape[1] // C):
        rows = slice(c * C, (c + 1) * C)
        a_low = a_ref[0, rows, a_off:a_off + GLA_RANK].astype(BF16)
        dk = [slice(h * DK, (h + 1) * DK) for h in heads]
        dv = [slice(h * DV, (h + 1) * DV) for h in heads]
        dvh = [slice((h % half) * DV, (h % half + 1) * DV) for h in heads]
        z = [_dot(a_low, wa[:, dk[h]]) + ba_ref[:, dk[h]] for h in heads]
        la = [(jnp.minimum(z[h], 0.0) - jnp.log1p(jnp.exp(-jnp.abs(z[h])))) / GLA_TAU for h in heads]
        hi = [la[h].astype(BF16) for h in heads]
        r1 = [la[h] - hi[h].astype(F32) for h in heads]
        mid = [r1[h].astype(BF16) for h in heads]
        lo = [(r1[h] - mid[h].astype(F32)).astype(BF16) for h in heads]
        bcum = [_dot(tri, hi[h]) + _dot(tri, mid[h]) + _dot(tri, lo[h]) for h in heads]
        blast = [bcum[h][C - 1:C, :] for h in heads]
        kh = [k_ref[0, rows, dk[h]] for h in heads]
        q_in = [(q_ref[0, rows, dk[h]] * (DK ** -0.5) * jnp.exp(bcum[h])).astype(BF16) for h in heads]
        k_in = [(kh[h] * jnp.exp(-bcum[h])).astype(BF16) for h in heads]
        k_out = [(kh[h] * jnp.exp(blast[h] - bcum[h])).astype(BF16) for h in heads]
        vb = [v_refs[h // half][0, rows, dvh[h]].astype(BF16) for h in heads]
        a_intra = [jnp.where(causal, _dot_nt(q_in[h], k_in[h]), 0.0).astype(BF16) for h in heads]
        state = [state_ref[h] for h in heads]
        o = [_dot(a_intra[h], vb[h]) + _dot_nt(q_in[h], state[h].astype(BF16)) for h in heads]
        upd = [_dot_tn(vb[h], k_out[h]) for h in heads]
        for h in heads:
            state_ref[h] = state[h] * jnp.exp(blast[h]) + upd[h]
            y = o[h] * lax.rsqrt(jnp.mean(o[h] * o[h], axis=-1, keepdims=True) + 1e-6) * ng_ref[...]
            o_ref[0, rows, dv[h]] = (y * _silu(r_refs[h // half][0, rows, dvh[h]])).astype(o_ref.dtype)


def _gla(proj, proj_small, a_off, w_a2, b_a, norm_g, q_col):
    B, S, _ = proj.shape
    H, DK, DV, TC = GLA_HEADS, GLA_DK, GLA_DV, GLA_TC
    hv = H * DV // 2
    qb = q_col // (H * DK)
    vb = (q_col + 2 * H * DK) // hv
    assert q_col % (H * DK) == 0 and (q_col + 2 * H * DK) % hv == 0
    return pl.pallas_call(
        functools.partial(_gla_kernel, a_off=a_off),
        out_shape=jax.ShapeDtypeStruct((B, S, H * DV), BF16),
        grid=(B, S // TC),
        in_specs=[pl.BlockSpec((1, TC, H * DK), lambda b, c: (b, c, qb)),
                  pl.BlockSpec((1, TC, H * DK), lambda b, c: (b, c, qb + 1)),
                  pl.BlockSpec((1, TC, hv), lambda b, c: (b, c, vb)),
                  pl.BlockSpec((1, TC, hv), lambda b, c: (b, c, vb + 1)),
                  pl.BlockSpec((1, TC, hv), lambda b, c: (b, c, vb + 2)),
                  pl.BlockSpec((1, TC, hv), lambda b, c: (b, c, vb + 3)),
                  pl.BlockSpec((1, TC, LANES), lambda b, c: (b, c, 0)),
                  pl.BlockSpec((GLA_RANK, H * DK), lambda b, c: (0, 0)),
                  pl.BlockSpec((1, H * DK), lambda b, c: (0, 0)),
                  pl.BlockSpec((1, DV), lambda b, c: (0, 0))],
        out_specs=pl.BlockSpec((1, TC, H * DV), lambda b, c: (b, c, 0)),
        scratch_shapes=[pltpu.VMEM((H, DV, DK), F32)],
        compiler_params=_cp("parallel", "arbitrary"),
        name="gla",
    )(proj, proj, proj, proj, proj, proj, proj_small, w_a2, b_a.reshape(1, H * DK), norm_g.reshape(1, DV))


def _conv_ln_kernel(u_ref, halo_ref, w_ref, b_ref, g_ref, beta_ref, o_ref, cat_ref, y_ref):
    TS, HALO, CH = CONV_TS, CONV_HALO, CONV_LANE_CHUNK
    D = u_ref.shape[2]
    n_chunks = D // CH
    first = HALO - (CONV_WIDTH - 1)
    ext = TS + SUBLANES

    @pl.when(pl.program_id(1) == 0)
    def _():
        cat_ref[0:HALO, :] = jnp.zeros((HALO, D), F32)

    @pl.when(pl.program_id(1) > 0)
    def _():
        cat_ref[0:HALO, :] = halo_ref[0]

    cat_ref[HALO:HALO + TS, :] = u_ref[0]
    cat_ref[HALO + TS:HALO + ext, :] = jnp.zeros((SUBLANES, D), F32)

    def conv_chunk(c, total):
        lanes = pl.ds(pl.multiple_of(c * CH, CH), CH)
        acc = jnp.zeros((TS, CH), F32) + b_ref[:, lanes]
        for r in range(SUBLANES):
            part = None
            for a in range((first + CONV_WIDTH - 1) // SUBLANES + 1):
                k = SUBLANES * a + r - first
                if 0 <= k < CONV_WIDTH:
                    term = cat_ref[pl.ds(SUBLANES * a, ext), lanes] * w_ref[pl.ds(k, 1), lanes]
                    part = term if part is None else part + term
            acc = acc + part[r:r + TS]
        y_ref[:, lanes] = acc
        return total + acc

    total = lax.fori_loop(0, n_chunks, conv_chunk, jnp.zeros((TS, CH), F32))
    mu = jnp.broadcast_to(jnp.sum(total, axis=-1, keepdims=True) / D, (TS, CH))

    def var_chunk(c, sq):
        d = y_ref[:, pl.ds(pl.multiple_of(c * CH, CH), CH)] - mu
        return sq + d * d

    sq = lax.fori_loop(0, n_chunks, var_chunk, jnp.zeros((TS, CH), F32), unroll=2)
    inv = jnp.broadcast_to(lax.rsqrt(jnp.sum(sq, axis=-1, keepdims=True) / D + 1e-5), (TS, CH))

    def out_chunk(c, carry):
        lanes = pl.ds(pl.multiple_of(c * CH, CH), CH)
        z = (y_ref[:, lanes] - mu) * inv * g_ref[:, lanes] + beta_ref[:, lanes]
        o_ref[0, :, lanes] = _silu(z).astype(o_ref.dtype)
        return carry

    lax.fori_loop(0, n_chunks, out_chunk, 0, unroll=2)


def _conv_ln_silu(u, w_dw, b_dw, ln_g, ln_b):
    B, S, D = u.shape
    TS, HALO = CONV_TS, CONV_HALO
    ratio = TS // HALO
    vec = pl.BlockSpec((1, D), lambda b, s: (0, 0))
    return pl.pallas_call(
        _conv_ln_kernel,
        out_shape=jax.ShapeDtypeStruct((B, S, D), BF16),
        grid=(B, S // TS),
        in_specs=[pl.BlockSpec((1, TS, D), lambda b, s: (b, s, 0)),
                  pl.BlockSpec((1, HALO, D), lambda b, s: (b, jnp.maximum(s * ratio - 1, 0), 0)),
                  pl.BlockSpec((CONV_WIDTH, D), lambda b, s: (0, 0)),
                  vec, vec, vec],
        out_specs=pl.BlockSpec((1, TS, D), lambda b, s: (b, s, 0)),
        scratch_shapes=[pltpu.VMEM((HALO + TS + SUBLANES, D), F32), pltpu.VMEM((TS, D), F32)],
        compiler_params=_cp("parallel", "parallel"),
        name="conv_ln_silu",
    )(u, u, w_dw, b_dw.reshape(1, D), ln_g.reshape(1, D), ln_b.reshape(1, D))


def _hybrid_attention(h, x2, gate, j, w_in, w_out, q_norm_g, k_norm_g, cmp_pos, cmp_w1, cmp_w2,
                      gla_w_a2, gla_b_a, gla_norm_g, B, S):
    D = h.shape[1]
    n_gate = NSA_HEADS * 3
    o_gl = NSA_Q + 6 * NSA_KV
    o_gq = o_gl + n_gate
    o_ga = o_gq + 2 * GLA_HEADS * GLA_DK + GLA_HEADS * GLA_DV
    o_gr = o_ga + GLA_RANK
    w_main, w_small = _regroup_w_in(w_in, j, ((0, o_gl), (o_gq, o_ga), (o_gr, w_in.shape[2])),
                                    ((o_gl, o_gq), (o_ga, o_gr)), 128)
    proj = _matmul(h, w_main, F32, 1024, 512).reshape(B, S, -1)
    proj_small = _matmul(h, w_small, F32, 1024, LANES).reshape(B, S, LANES)

    t = jnp.arange(S, dtype=jnp.int32)
    cmp_end = jnp.arange(S // CMP_STRIDE, dtype=jnp.int32) * CMP_STRIDE + (CMP_BLOCK - 1)
    qn, ksn, vs_t, kwn, vw_t = _nsa_prep(proj, _rope_tables(t), q_norm_g, k_norm_g)
    kcmp, vcmp_t = _nsa_compress(proj, cmp_pos, cmp_w1, cmp_w2, k_norm_g, _rope_tables(cmp_end))
    gate_logits_t = proj_small[:, :, :n_gate].reshape(B, S, NSA_KV_GROUPS, 3 * NSA_HPG).transpose(0, 2, 3, 1)
    o_nsa = _nsa_attention(qn, kcmp, vcmp_t, ksn, vs_t, kwn, vw_t, gate_logits_t)
    o_gla = _gla(proj, proj_small, n_gate, gla_w_a2, gla_b_a, gla_norm_g, o_gl)
    xs = (o_nsa.reshape(B * S, -1), o_gla.reshape(B * S, -1))
    return _matmul_residual(xs, w_out, j, jnp.zeros((1, D), F32), x2, gate, S, 1024, 512)


def _conformer(h, x2, gate, j, w_pw1, b_pw1, w_dw, b_dw, ln_g, ln_b, w_pw2, b_pw2, B, S):
    D = h.shape[1]
    tn = 256
    b_pair = b_pw1.reshape(2, D // tn, tn).transpose(1, 0, 2).reshape(1, 2 * D)
    u = _matmul_glu_pair(h, _pair_cast(w_pw1, j, tn, 128), b_pair, F32, 1024, tn)
    v = _conv_ln_silu(u.reshape(B, S, D), w_dw, b_dw, ln_g, ln_b).reshape(B * S, D)
    return _matmul_residual((v,), w_pw2, j, b_pw2.reshape(1, D), x2, gate, S, 1024, 512)


def _swiglu(h, x2, gate, layer, w_gate, w_up, w_down, S):
    D = h.shape[1]
    act = _matmul_swiglu(h, w_gate, w_up, layer, BF16, 1024, 256)
    return _matmul_residual((act,), _cast_bf16(w_down, layer, 256), None, jnp.zeros((1, D), F32), x2, gate, S,
                            1024, 256)


def kernel(x, c, w_mod, b_mod, ada_table, norm_mix_g, norm_ffn_g, w_in, w_out, q_norm_g, k_norm_g, cmp_pos, cmp_w1, cmp_w2, gla_w_a2, gla_b_a, gla_norm_g, cv_w_pw1, cv_b_pw1, cv_w_dw, cv_b_dw, cv_ln_g, cv_ln_b, cv_w_pw2, cv_b_pw2, ffn_w_gate, ffn_w_up, ffn_w_down):
    B, S, D = x.shape
    depth = ada_table.shape[0]
    mod = _ada_mod(c, w_mod, b_mod).reshape(B, N_MOD, D)
    for layer in range(depth):
        m = mod + ada_table[layer]
        sh_a, sc_a, g_a, sh_f, sc_f, g_f = [m[:, i, :] for i in range(N_MOD)]
        h = _norm_mod(x, norm_mix_g[layer], sc_a, sh_a).reshape(B * S, D)
        x2 = x.reshape(B * S, D)
        g_a3 = g_a.reshape(B, 1, D)
        j = layer // 2
        if layer % 2 == 0:
            x2 = _hybrid_attention(h, x2, g_a3, j, w_in, w_out, q_norm_g[j], k_norm_g[j], cmp_pos[j],
                                   cmp_w1[j], cmp_w2[j], gla_w_a2[j], gla_b_a[j], gla_norm_g[j], B, S)
        else:
            x2 = _conformer(h, x2, g_a3, j, cv_w_pw1, cv_b_pw1[j], cv_w_dw[j], cv_b_dw[j], cv_ln_g[j],
                            cv_ln_b[j], cv_w_pw2, cv_b_pw2[j], B, S)
        x = x2.reshape(B, S, D)
        h = _norm_mod(x, norm_ffn_g[layer], sc_f, sh_f).reshape(B * S, D)
        x2 = _swiglu(h, x2, g_f.reshape(B, 1, D), layer, ffn_w_gate, ffn_w_up, ffn_w_down, S)
        x = x2.reshape(B, S, D)
    return x
```

```python
import functools

import numpy as np
import jax
import jax.numpy as jnp
from jax import lax
from jax.experimental import pallas as pl
from jax.experimental.pallas import tpu as pltpu

F32 = jnp.float32
BF16 = jnp.bfloat16

HEAD_DIM = 128
NSA_HEADS = 16
NSA_KV_GROUPS = 4
NSA_HPG = NSA_HEADS // NSA_KV_GROUPS
CMP_BLOCK = 32
CMP_STRIDE = 16
SLC_BLOCK = 64
SLC_TOPK = 16
WINDOW = 512
ROPE_THETA = 500000.0
ROPE_DIM = HEAD_DIM // 4
GLA_HEADS = 4
GLA_DK = 256
GLA_DV = 512
GLA_RANK = 16
GLA_TAU = 16.0
GLA_CHUNK = 64
CONV_WIDTH = 31
N_MOD = 6
NSA_Q = NSA_HEADS * HEAD_DIM
NSA_KV = NSA_KV_GROUPS * HEAD_DIM

VMEM_LIMIT_BYTES = 56 * 1024 * 1024
LANES = 128
SUBLANES = 8
NEG_BIG = -1e30
M_INIT = -1e29

ATTN_TQ = 128
ATTN_UNROLL = 4
CONV_TS = 128
CONV_HALO = 32
CONV_LANE_CHUNK = 128
GLA_TC = 128


def _cp(*sem):
    return pltpu.CompilerParams(dimension_semantics=sem, vmem_limit_bytes=VMEM_LIMIT_BYTES)


def _dot(a, b):
    return jnp.dot(a, b, preferred_element_type=F32)


def _dot_nt(a, b):
    return lax.dot_general(a, b, (((1,), (1,)), ((), ())), preferred_element_type=F32)


def _dot_tn(a, b):
    return lax.dot_general(a, b, (((0,), (0,)), ((), ())), preferred_element_type=F32)


def _silu(x):
    return x * jax.nn.sigmoid(x)


def _cast_weights_once(w_refs, wbf_refs):
    @pl.when(pl.program_id(1) == 0)
    def _():
        for w_ref, wbf_ref in zip(w_refs, wbf_refs):
            wbf_ref[...] = w_ref[...].astype(BF16)


def _mm_kernel(x_ref, w_ref, o_ref):
    o_ref[...] = _dot(x_ref[...], w_ref[...]).astype(o_ref.dtype)


def _matmul(x, w, out_dtype, tm, tn):
    M, K = x.shape
    N = w.shape[1]
    return pl.pallas_call(
        _mm_kernel,
        out_shape=jax.ShapeDtypeStruct((M, N), out_dtype),
        grid=(M // tm, N // tn),
        in_specs=[pl.BlockSpec((tm, K), lambda i, j: (i, 0)),
                  pl.BlockSpec((K, tn), lambda i, j: (0, j))],
        out_specs=pl.BlockSpec((tm, tn), lambda i, j: (i, j)),
        compiler_params=_cp("parallel", "parallel"),
        name="matmul",
    )(x, w)


def _mm_swiglu_kernel(x_ref, wg_ref, wu_ref, o_ref, wgbf_ref, wubf_ref):
    _cast_weights_once((wg_ref, wu_ref), (wgbf_ref, wubf_ref))
    x = x_ref[...]
    o_ref[...] = (_silu(_dot(x, wgbf_ref[...])) * _dot(x, wubf_ref[...])).astype(o_ref.dtype)


def _matmul_swiglu(x, w_gate, w_up, layer, out_dtype, tm, tn):
    M, K = x.shape
    N = w_gate.shape[2]
    w_spec = pl.BlockSpec((None, K, tn), lambda j, i: (layer, 0, j))
    return pl.pallas_call(
        _mm_swiglu_kernel,
        out_shape=jax.ShapeDtypeStruct((M, N), out_dtype),
        grid=(N // tn, M // tm),
        in_specs=[pl.BlockSpec((tm, K), lambda j, i: (i, 0)), w_spec, w_spec],
        out_specs=pl.BlockSpec((tm, tn), lambda j, i: (i, j)),
        scratch_shapes=[pltpu.VMEM((K, tn), BF16), pltpu.VMEM((K, tn), BF16)],
        compiler_params=_cp("arbitrary", "arbitrary"),
        name="matmul_swiglu",
    )(x, w_gate, w_up)


def _mm_glu_pair_kernel(x_ref, w_ref, b_ref, o_ref):
    tn = o_ref.shape[1]
    x = x_ref[...]
    a = _dot(x, w_ref[:, :tn]) + b_ref[:, :tn]
    b = _dot(x, w_ref[:, tn:]) + b_ref[:, tn:]
    o_ref[...] = (a * jax.nn.sigmoid(b)).astype(o_ref.dtype)


def _matmul_glu_pair(x, w_pair, b_pair, out_dtype, tm, tn):
    M, K = x.shape
    N = w_pair.shape[1] // 2
    return pl.pallas_call(
        _mm_glu_pair_kernel,
        out_shape=jax.ShapeDtypeStruct((M, N), out_dtype),
        grid=(M // tm, N // tn),
        in_specs=[pl.BlockSpec((tm, K), lambda i, j: (i, 0)),
                  pl.BlockSpec((K, 2 * tn), lambda i, j: (0, j)),
                  pl.BlockSpec((1, 2 * tn), lambda i, j: (0, j))],
        out_specs=pl.BlockSpec((tm, tn), lambda i, j: (i, j)),
        compiler_params=_cp("parallel", "parallel"),
        name="matmul_glu_pair",
    )(x, w_pair, b_pair)


def _pair_cast_kernel(w_ref, o_ref, *, tn):
    half = w_ref.shape[1] // 2
    for j in range(half // tn):
        o_ref[:, 2 * j * tn:(2 * j + 1) * tn] = w_ref[:, j * tn:(j + 1) * tn].astype(BF16)
        o_ref[:, (2 * j + 1) * tn:(2 * j + 2) * tn] = w_ref[:, half + j * tn:half + (j + 1) * tn].astype(BF16)


def _pair_cast(w, layer, tn, rows):
    _, K, N2 = w.shape
    return pl.pallas_call(
        functools.partial(_pair_cast_kernel, tn=tn),
        out_shape=jax.ShapeDtypeStruct((K, N2), BF16),
        grid=(K // rows,),
        in_specs=[pl.BlockSpec((None, rows, N2), lambda i: (layer, i, 0))],
        out_specs=pl.BlockSpec((rows, N2), lambda i: (i, 0)),
        compiler_params=_cp("parallel"),
        name="pair_cast",
    )(w)


def _mm_res_kernel(*refs, n_x, cast):
    x_refs = refs[:n_x]
    if cast:
        w_ref, b_ref, res_ref, g_ref, o_ref, wbf_ref = refs[n_x:]
        _cast_weights_once((w_ref,), (wbf_ref,))
    else:
        wbf_ref, b_ref, res_ref, g_ref, o_ref = refs[n_x:]
    y = b_ref[...]
    off = 0
    for x_ref in x_refs:
        k = x_ref.shape[1]
        y = y + _dot(x_ref[...], wbf_ref[off:off + k, :])
        off += k
    o_ref[...] = res_ref[...] + g_ref[0] * y


def _matmul_residual(xs, w, layer, bias, res, gate, rows_per_batch, tm, tn):
    M = xs[0].shape[0]
    K, N = w.shape[-2:]
    bpt = rows_per_batch // tm
    cast = w.ndim == 3
    w_block = (None, K, tn) if cast else (K, tn)
    w_index = (lambda i, j: (layer, 0, j)) if cast else (lambda i, j: (0, j))
    if cast:
        grid = (N // tn, M // tm)
        ij = lambda f: (lambda j, i: f(i, j))
        x_mode, w_mode = None, pl.Buffered(1)
        scratch = [pltpu.VMEM((K, tn), BF16)]
    else:
        grid = (M // tm, N // tn)
        ij = lambda f: f
        x_mode, w_mode = pl.Buffered(1), None
        scratch = []
    return pl.pallas_call(
        functools.partial(_mm_res_kernel, n_x=len(xs), cast=cast),
        out_shape=jax.ShapeDtypeStruct((M, N), F32),
        grid=grid,
        in_specs=[pl.BlockSpec((tm, x.shape[1]), ij(lambda i, j: (i, 0)), pipeline_mode=x_mode) for x in xs] + [
                  pl.BlockSpec(w_block, ij(w_index), pipeline_mode=w_mode),
                  pl.BlockSpec((1, tn), ij(lambda i, j: (0, j))),
                  pl.BlockSpec((tm, tn), ij(lambda i, j: (i, j))),
                  pl.BlockSpec((1, 1, tn), ij(lambda i, j: (i // bpt, 0, j)))],
        out_specs=pl.BlockSpec((tm, tn), ij(lambda i, j: (i, j))),
        scratch_shapes=scratch,
        compiler_params=_cp("arbitrary", "arbitrary"),
        name="matmul_residual",
    )(*xs, w, bias, res, gate)


def _cast_kernel(w_ref, o_ref):
    o_ref[...] = w_ref[...].astype(o_ref.dtype)


def _cast_bf16(w, layer, rows):
    _, K, N = w.shape
    return pl.pallas_call(
        _cast_kernel,
        out_shape=jax.ShapeDtypeStruct((K, N), BF16),
        grid=(K // rows,),
        in_specs=[pl.BlockSpec((None, rows, N), lambda i: (layer, i, 0))],
        out_specs=pl.BlockSpec((rows, N), lambda i: (i, 0)),
        compiler_params=_cp("parallel"),
        name="cast_bf16",
    )(w)


def _regroup_kernel(wt_ref, main_ref, small_ref, *, segments, small_starts):
    off = 0
    for lo, hi in segments:
        for r in range(lo, hi, LANES):
            main_ref[:, off:off + LANES] = wt_ref[r:r + LANES, :].T.astype(BF16)
            off += LANES
    for n, lo in enumerate(small_starts):
        small_ref[:, n * LANES:(n + 1) * LANES] = wt_ref[lo:lo + LANES, :].T.astype(BF16)


def _regroup_w_in(w_in, layer, segments, small_starts, tk):
    wt = jnp.swapaxes(w_in, 1, 2)
    _, N, K = wt.shape
    assert all((hi - lo) % LANES == 0 and lo % SUBLANES == 0 for lo, hi in segments)
    assert all(lo % SUBLANES == 0 and lo + LANES <= N for lo in small_starts)
    n_main = sum(hi - lo for lo, hi in segments)
    n_small = LANES * len(small_starts)
    return pl.pallas_call(
        functools.partial(_regroup_kernel, segments=segments, small_starts=small_starts),
        out_shape=(jax.ShapeDtypeStruct((K, n_main), BF16), jax.ShapeDtypeStruct((K, n_small), BF16)),
        grid=(K // tk,),
        in_specs=[pl.BlockSpec((None, N, tk), lambda k: (layer, 0, k))],
        out_specs=(pl.BlockSpec((tk, n_main), lambda k: (k, 0)), pl.BlockSpec((tk, n_small), lambda k: (k, 0))),
        compiler_params=_cp("parallel"),
        name="regroup_w_in",
    )(wt)


def _mod_kernel(c_ref, w_ref, b_ref, o_ref):
    a = _silu(c_ref[...]).astype(BF16)
    o_ref[...] = _dot(a, w_ref[...].astype(BF16)) + b_ref[...]


def _ada_mod(c, w_mod, b_mod):
    B, D = c.shape
    N = w_mod.shape[1]
    tn = 512
    return pl.pallas_call(
        _mod_kernel,
        out_shape=jax.ShapeDtypeStruct((B, N), F32),
        grid=(N // tn,),
        in_specs=[pl.BlockSpec((B, D), lambda j: (0, 0)),
                  pl.BlockSpec((D, tn), lambda j: (0, j)),
                  pl.BlockSpec((1, tn), lambda j: (0, j))],
        out_specs=pl.BlockSpec((B, tn), lambda j: (0, j)),
        compiler_params=_cp("parallel"),
        name="ada_mod",
    )(c, w_mod, b_mod.reshape(1, N))


def _norm_mod_kernel(x_ref, g_ref, sc_ref, sh_ref, o_ref, gm_ref):
    rows_per_trip = 2 * SUBLANES
    gm_ref[...] = g_ref[...] * (1.0 + sc_ref[0])

    def trip(i, carry):
        rows = pl.ds(pl.multiple_of(i * rows_per_trip, rows_per_trip), rows_per_trip)
        x = x_ref[0, rows, :]
        r = lax.rsqrt(jnp.mean(x * x, axis=-1, keepdims=True) + 1e-6)
        o_ref[0, rows, :] = (x * r * gm_ref[...] + sh_ref[0]).astype(o_ref.dtype)
        return carry

    lax.fori_loop(0, x_ref.shape[1] // rows_per_trip, trip, 0, unroll=4)


def _norm_mod(x, g, sc, sh):
    B, S, D = x.shape
    ts = 512
    return pl.pallas_call(
        _norm_mod_kernel,
        out_shape=jax.ShapeDtypeStruct((B, S, D), BF16),
        grid=(B, S // ts),
        in_specs=[pl.BlockSpec((1, ts, D), lambda b, s: (b, s, 0)),
                  pl.BlockSpec((1, D), lambda b, s: (0, 0)),
                  pl.BlockSpec((1, 1, D), lambda b, s: (b, 0, 0)),
                  pl.BlockSpec((1, 1, D), lambda b, s: (b, 0, 0))],
        out_specs=pl.BlockSpec((1, ts, D), lambda b, s: (b, s, 0)),
        scratch_shapes=[pltpu.VMEM((1, D), F32)],
        compiler_params=_cp("parallel", "parallel"),
        name="norm_mod",
    )(x, g.reshape(1, D), sc.reshape(B, 1, D), sh.reshape(B, 1, D))


def _rope_tables(pos):
    half = ROPE_DIM // 2
    inv_freq = ROPE_THETA ** (-jnp.arange(half, dtype=F32) / half)
    ang = pos.astype(F32)[:, None] * inv_freq[None, :]
    cos, sin = jnp.cos(ang), jnp.sin(ang)
    n = pos.shape[0]
    rest = HEAD_DIM - ROPE_DIM
    c = jnp.concatenate([cos, cos, jnp.ones((n, rest), F32)], axis=-1)
    s_lo = jnp.concatenate([-sin, jnp.zeros((n, HEAD_DIM - half), F32)], axis=-1)
    s_hi = jnp.concatenate([jnp.zeros((n, half), F32), sin, jnp.zeros((n, rest), F32)], axis=-1)
    return c, s_lo, s_hi


def _norm_rope(x, g, c, s_lo, s_hi):
    y = x * lax.rsqrt(jnp.mean(x * x, axis=-1, keepdims=True) + 1e-6) * g
    half = ROPE_DIM // 2
    return y * c + pltpu.roll(y, HEAD_DIM - half, 1) * s_lo + pltpu.roll(y, half, 1) * s_hi


def _nsa_prep_kernel(q_ref, ks_ref, vs_ref, kw_ref, vw_ref, c_ref, slo_ref, shi_ref, qg_ref, kg_ref,
                     qo_ref, kso_ref, vso_ref, kwo_ref, vwo_ref):
    c, s_lo, s_hi = c_ref[...], slo_ref[...], shi_ref[...]
    scale = HEAD_DIM ** -0.5
    for h in range(NSA_HEADS):
        sl = slice(h * HEAD_DIM, (h + 1) * HEAD_DIM)
        qo_ref[0, :, sl] = (_norm_rope(q_ref[0, :, sl], qg_ref[...], c, s_lo, s_hi) * scale).astype(BF16)
    for g in range(NSA_KV_GROUPS):
        sl = slice(g * HEAD_DIM, (g + 1) * HEAD_DIM)
        kso_ref[0, :, sl] = _norm_rope(ks_ref[0, :, sl], kg_ref[1:2, :], c, s_lo, s_hi).astype(BF16)
        kwo_ref[0, :, sl] = _norm_rope(kw_ref[0, :, sl], kg_ref[2:3, :], c, s_lo, s_hi).astype(BF16)
    vso_ref[0] = vs_ref[0].T.astype(BF16)
    vwo_ref[0] = vw_ref[0].T.astype(BF16)


def _nsa_prep(proj, tables, q_norm_g, k_norm_g):
    B, S, _ = proj.shape
    ts = 256
    kvb = NSA_Q // NSA_KV

    def kv_spec(n):
        return pl.BlockSpec((1, ts, NSA_KV), lambda b, s: (b, s, kvb + n))

    tab = pl.BlockSpec((ts, HEAD_DIM), lambda b, s: (s, 0))
    out_k = pl.BlockSpec((1, ts, NSA_KV), lambda b, s: (b, s, 0))
    out_vt = pl.BlockSpec((1, NSA_KV, ts), lambda b, s: (b, 0, s))
    k_shape = jax.ShapeDtypeStruct((B, S, NSA_KV), BF16)
    vt_shape = jax.ShapeDtypeStruct((B, NSA_KV, S), BF16)
    return pl.pallas_call(
        _nsa_prep_kernel,
        out_shape=(jax.ShapeDtypeStruct((B, S, NSA_Q), BF16), k_shape, vt_shape, k_shape, vt_shape),
        grid=(B, S // ts),
        in_specs=[pl.BlockSpec((1, ts, NSA_Q), lambda b, s: (b, s, 0)),
                  kv_spec(2), kv_spec(3), kv_spec(4), kv_spec(5), tab, tab, tab,
                  pl.BlockSpec((1, HEAD_DIM), lambda b, s: (0, 0)),
                  pl.BlockSpec((3, HEAD_DIM), lambda b, s: (0, 0))],
        out_specs=(pl.BlockSpec((1, ts, NSA_Q), lambda b, s: (b, s, 0)), out_k, out_vt, out_k, out_vt),
        compiler_params=_cp("parallel", "parallel"),
        name="nsa_prep",
    )(proj, proj, proj, proj, proj, *tables, q_norm_g.reshape(1, HEAD_DIM), k_norm_g)


def _compress_kernel(kc_ref, vc_ref, pos_ref, w1_ref, w2_ref, kg_ref, c_ref, slo_ref, shi_ref,
                     ko_ref, vo_ref):
    half_blk = CMP_BLOCK // 2
    n_seg = kc_ref.shape[1] // CMP_STRIDE

    def compress(tok_ref, j):
        u = jnp.zeros((n_seg, w1_ref.shape[2]), F32)
        v = jnp.zeros((n_seg, w1_ref.shape[2]), F32)
        for l in range(half_blk):
            x = tok_ref[0, pl.ds(l, n_seg, stride=CMP_STRIDE), :]
            xa = (x + pos_ref[j, l:l + 1, :]).astype(BF16)
            xb = (x + pos_ref[j, half_blk + l:half_blk + l + 1, :]).astype(BF16)
            u = u + _dot(xa, w1_ref[j, l * HEAD_DIM:(l + 1) * HEAD_DIM, :])
            v = v + _dot(xb, w1_ref[j, (half_blk + l) * HEAD_DIM:(half_blk + l + 1) * HEAD_DIM, :])
        h = u + pltpu.roll(v, n_seg - 1, 0)
        return _dot(jax.nn.gelu(h).astype(BF16), w2_ref[j])

    k = compress(kc_ref, 0)
    ko_ref[0, 0] = _norm_rope(k, kg_ref[0:1, :], c_ref[...], slo_ref[...], shi_ref[...]).astype(BF16)
    vo_ref[0, 0] = compress(vc_ref, 1).T.astype(BF16)


def _nsa_compress(proj, cmp_pos, cmp_w1, cmp_w2, k_norm_g, cmp_tables):
    B, S, _ = proj.shape
    G = NSA_KV_GROUPS
    n_seg = S // CMP_STRIDE
    kcb = NSA_Q // HEAD_DIM
    full2 = lambda b, g: (0, 0)
    full3 = lambda b, g: (0, 0, 0)
    return pl.pallas_call(
        _compress_kernel,
        out_shape=(jax.ShapeDtypeStruct((B, G, n_seg, HEAD_DIM), BF16),
                   jax.ShapeDtypeStruct((B, G, HEAD_DIM, n_seg), BF16)),
        grid=(B, G),
        in_specs=[pl.BlockSpec((1, S, HEAD_DIM), lambda b, g: (b, 0, kcb + g)),
                  pl.BlockSpec((1, S, HEAD_DIM), lambda b, g: (b, 0, kcb + G + g)),
                  pl.BlockSpec(cmp_pos.shape, full3),
                  pl.BlockSpec(cmp_w1.shape, full3),
                  pl.BlockSpec(cmp_w2.shape, full3),
                  pl.BlockSpec((3, HEAD_DIM), full2),
                  pl.BlockSpec((n_seg, HEAD_DIM), full2),
                  pl.BlockSpec((n_seg, HEAD_DIM), full2),
                  pl.BlockSpec((n_seg, HEAD_DIM), full2)],
        out_specs=(pl.BlockSpec((1, 1, n_seg, HEAD_DIM), lambda b, g: (b, g, 0, 0)),
                   pl.BlockSpec((1, 1, HEAD_DIM, n_seg), lambda b, g: (b, g, 0, 0))),
        compiler_params=_cp("parallel", "parallel"),
        name="nsa_compress",
    )(proj, proj, cmp_pos, cmp_w1.astype(BF16), cmp_w2.astype(BF16), k_norm_g, *cmp_tables)


def _fold_rows(x, op):
    return op(x.reshape(x.shape[0] // SUBLANES, SUBLANES, x.shape[1]), axis=0)


def _nsa_attn_kernel(q_ref, kc_ref, vct_ref, ks_ref, vst_ref, kw_ref, vwt_ref, gate_ref, ovl_ref, exp_ref,
                     o_ref, selm_ref, ss_ref, sw_ref):
    TQ, TK = ATTN_TQ, LANES
    R = NSA_HPG * TQ
    qi = pl.program_id(2)
    t0 = qi * TQ
    q = jnp.concatenate([q_ref[0, :, h * HEAD_DIM:(h + 1) * HEAD_DIM] for h in range(NSA_HPG)], axis=0)
    tq = t0 + (lax.broadcasted_iota(jnp.int32, (TK, R), 1) & (TQ - 1))
    key = lax.broadcasted_iota(jnp.int32, (TK, R), 0)

    def score_tile(k_ref, scr_ref, slot, kt, m8, use_sel=False, causal=False, window=False, valid=None):
        off = pl.multiple_of(kt * TK, TK)
        s = _dot_nt(k_ref[0, pl.ds(off, TK), :], q)
        mask = None
        if use_sel:
            sm = selm_ref[pl.ds(off, TK), :]
            mask = jnp.concatenate([sm] * NSA_HPG, axis=1) > 0.5
        if causal:
            c = (off + key) <= tq
            mask = c if mask is None else mask & c
        if window:
            w = (tq - (off + key)) < WINDOW
            mask = w if mask is None else mask & w
        if valid is not None:
            mask = valid if mask is None else mask & valid
        if mask is not None:
            s = jnp.where(mask, s, NEG_BIG)
        scr_ref[slot] = s
        return jnp.maximum(m8, _fold_rows(s, jnp.max))

    def value_tile(scr_ref, vt_ref, slot, kt, m, carry):
        l8, acc = carry
        off = pl.multiple_of(kt * TK, TK)
        p = jnp.exp(scr_ref[slot] - m)
        return l8 + _fold_rows(p, jnp.sum), acc + _dot(vt_ref[0, :, pl.ds(off, TK)], p.astype(BF16))

    def finish(carry):
        l8, acc = carry
        l = jnp.sum(l8, axis=0, keepdims=True)
        return acc / jnp.where(l > 0.0, l, 1.0)

    m_init = jnp.full((SUBLANES, R), M_INIT, F32)
    acc_init = (jnp.zeros((SUBLANES, R), F32), jnp.zeros((HEAD_DIM, R), F32))

    n_back = WINDOW // TK
    win_tiles = [(jnp.maximum(qi - n_back + u, 0), qi - n_back + u >= 0) for u in range(n_back)]
    m8_w = m_init
    for u, (kt, valid) in enumerate(win_tiles):
        m8_w = score_tile(kw_ref, sw_ref, u, kt, m8_w, window=(u == 0), valid=valid)
    m8_w = score_tile(kw_ref, sw_ref, n_back, qi, m8_w, causal=True)
    m_w = jnp.max(m8_w, axis=0, keepdims=True)

    s = _dot_nt(kc_ref[0, 0], q)
    mask = (key * CMP_STRIDE + (CMP_BLOCK - 1)) <= tq
    s = jnp.where(mask, s, NEG_BIG)
    p = jnp.where(mask, jnp.exp(s - jnp.max(s, axis=0, keepdims=True)), 0.0)
    l = jnp.sum(p, axis=0, keepdims=True)
    pb = (p / jnp.where(l > 0.0, l, 1.0)).astype(BF16)
    o_cmp = _dot(vct_ref[0, 0], pb)

    c_w = acc_init
    for u, (kt, _) in enumerate(win_tiles):
        c_w = value_tile(sw_ref, vwt_ref, u, kt, m_w, c_w)
    o_win = finish(value_tile(sw_ref, vwt_ref, n_back, qi, m_w, c_w))

    n_slc = ovl_ref.shape[0]
    ranked = t0 + TQ > SLC_TOPK * SLC_BLOCK
    last_group = qi // ATTN_UNROLL
    group_keys = ATTN_UNROLL * TK

    @pl.when(ranked)
    def _():
        imp_heads = _dot(ovl_ref[...], pb)
        imp = imp_heads[:, 0:TQ]
        for h in range(1, NSA_HPG):
            imp = imp + imp_heads[:, h * TQ:(h + 1) * TQ]
        t = t0 + lax.broadcasted_iota(jnp.int32, (n_slc, TQ), 1)
        blk = lax.broadcasted_iota(jnp.int32, (n_slc, TQ), 0)
        cur = t // SLC_BLOCK
        forced = (blk == 0) | (blk == cur) | (blk == cur - 1)
        valid = blk * SLC_BLOCK <= t
        val = jnp.where(forced, jnp.inf, jnp.where(valid, imp, -jnp.inf))
        rank = jnp.zeros((n_slc, TQ), F32)
        for i in range(n_slc):
            vi = val[i:i + 1, :]
            ahead = (vi > val) | ((vi == val) & (blk > i))
            rank = rank + jnp.where(ahead, 1.0, 0.0)
        sel = jnp.where((rank < float(SLC_TOPK)) & (val > -jnp.inf), 1.0, 0.0).astype(BF16)
        selm_ref[...] = _dot(exp_ref[...], sel)

    @pl.when(jnp.logical_not(ranked))
    def _():
        rows = pl.ds(pl.multiple_of(last_group * group_keys, group_keys), group_keys)
        selm_ref[rows, :] = jnp.ones((group_keys, TQ), F32)

    def score_group(a, m8, use_sel, causal):
        for u in range(ATTN_UNROLL):
            kt = a * ATTN_UNROLL + u
            m8 = score_tile(ks_ref, ss_ref, kt, kt, m8, use_sel=use_sel, causal=causal)
        return m8

    n_masked = jnp.where(ranked, last_group, 0)
    m8 = lax.fori_loop(0, n_masked, lambda a, m: score_group(a, m, True, False), m_init)
    m8 = lax.fori_loop(0, last_group - n_masked, lambda a, m: score_group(a, m, False, False), m8)
    m8_s = score_group(last_group, m8, True, True)
    m_s = jnp.max(m8_s, axis=0, keepdims=True)

    def value_group(a, carry):
        for u in range(ATTN_UNROLL):
            kt = a * ATTN_UNROLL + u
            carry = value_tile(ss_ref, vst_ref, kt, kt, m_s, carry)
        return carry

    o_slc = finish(lax.fori_loop(0, last_group + 1, value_group, acc_init))

    gate = jax.nn.sigmoid(gate_ref[0, 0])
    for h in range(NSA_HPG):
        cols = slice(h * TQ, (h + 1) * TQ)
        o = (gate[3 * h:3 * h + 1, :] * o_cmp[:, cols] + gate[3 * h + 1:3 * h + 2, :] * o_slc[:, cols]
             + gate[3 * h + 2:3 * h + 3, :] * o_win[:, cols])
        o_ref[0, :, h * HEAD_DIM:(h + 1) * HEAD_DIM] = o.T.astype(o_ref.dtype)


def _nsa_attention(qn, kcmp, vcmp_t, ksn, vs_t, kwn, vw_t, gate_logits_t):
    B, S, _ = qn.shape
    G = NSA_KV_GROUPS
    TQ = ATTN_TQ
    n_cmp = S // CMP_STRIDE
    n_slc = S // SLC_BLOCK
    assert n_cmp == LANES and WINDOW % LANES == 0 and TQ == LANES
    cmp_start = np.arange(n_cmp) * CMP_STRIDE
    slc_start = np.arange(n_slc) * SLC_BLOCK
    overlap = np.clip(np.minimum(cmp_start[None, :] + CMP_BLOCK, slc_start[:, None] + SLC_BLOCK)
                      - np.maximum(cmp_start[None, :], slc_start[:, None]), 0, None) / CMP_STRIDE
    ovl = jnp.asarray(overlap, dtype=BF16)
    expand = jnp.asarray((np.arange(S)[:, None] // SLC_BLOCK) == np.arange(n_slc)[None, :], dtype=BF16)
    gq = NSA_HPG * HEAD_DIM
    q_spec = pl.BlockSpec((1, TQ, gq), lambda b, g, i: (b, i, g))
    k_spec = pl.BlockSpec((1, S, HEAD_DIM), lambda b, g, i: (b, 0, g))
    vt_spec = pl.BlockSpec((1, HEAD_DIM, S), lambda b, g, i: (b, g, 0))
    return pl.pallas_call(
        _nsa_attn_kernel,
        out_shape=jax.ShapeDtypeStruct((B, S, NSA_Q), BF16),
        grid=(B, G, S // TQ),
        in_specs=[q_spec,
                  pl.BlockSpec((1, 1, n_cmp, HEAD_DIM), lambda b, g, i: (b, g, 0, 0)),
                  pl.BlockSpec((1, 1, HEAD_DIM, n_cmp), lambda b, g, i: (b, g, 0, 0)),
                  k_spec, vt_spec, k_spec, vt_spec,
                  pl.BlockSpec((1, 1, 3 * NSA_HPG, TQ), lambda b, g, i: (b, g, 0, i)),
                  pl.BlockSpec(ovl.shape, lambda b, g, i: (0, 0)),
                  pl.BlockSpec(expand.shape, lambda b, g, i: (0, 0))],
        out_specs=q_spec,
        scratch_shapes=[pltpu.VMEM((S, TQ), F32),
                        pltpu.VMEM((S // LANES, LANES, NSA_HPG * TQ), F32),
                        pltpu.VMEM((WINDOW // LANES + 1, LANES, NSA_HPG * TQ), F32)],
        compiler_params=_cp("parallel", "parallel", "parallel"),
        name="nsa_attention",
    )(qn, kcmp, vcmp_t, ksn, vs_t, kwn, vw_t, gate_logits_t, ovl, expand)


def _gla_kernel(q_ref, k_ref, v0_ref, v1_ref, r0_ref, r1_ref, a_ref, wa_ref, ba_ref, ng_ref, o_ref, state_ref):
    C, DK, DV = GLA_CHUNK, GLA_DK, GLA_DV
    half = GLA_HEADS // 2
    v_refs, r_refs = (v0_ref, v1_ref), (r0_ref, r1_ref)

    @pl.when(pl.program_id(1) == 0)
    def _():
        state_ref[...] = jnp.zeros_like(state_ref)

    row = lax.broadcasted_iota(jnp.int32, (C, C), 0)
    colc = lax.broadcasted_iota(jnp.int32, (C, C), 1)
    causal = colc <= row
    tri = jnp.where(causal, 1.0, 0.0).astype(BF16)
    wa = wa_ref[...].astype(BF16)
    heads = range(GLA_HEADS)
    for c in range(q_ref.shape[1] // C):
        rows = slice(c * C, (c + 1) * C)
        a_low = a_ref[0, rows, 0:GLA_RANK].astype(BF16)
        dk = [slice(h * DK, (h + 1) * DK) for h in heads]
        dv = [slice(h * DV, (h + 1) * DV) for h in heads]
        dvh = [slice((h % half) * DV, (h % half + 1) * DV) for h in heads]
        z = [_dot(a_low, wa[:, dk[h]]) + ba_ref[:, dk[h]] for h in heads]
        la = [(jnp.minimum(z[h], 0.0) - jnp.log1p(jnp.exp(-jnp.abs(z[h])))) / GLA_TAU for h in heads]
        hi = [la[h].astype(BF16) for h in heads]
        r1 = [la[h] - hi[h].astype(F32) for h in heads]
        mid = [r1[h].astype(BF16) for h in heads]
        lo = [(r1[h] - mid[h].astype(F32)).astype(BF16) for h in heads]
        bcum = [_dot(tri, hi[h]) + _dot(tri, mid[h]) + _dot(tri, lo[h]) for h in heads]
        blast = [bcum[h][C - 1:C, :] for h in heads]
        kh = [k_ref[0, rows, dk[h]] for h in heads]
        q_in = [(q_ref[0, rows, dk[h]] * (DK ** -0.5) * jnp.exp(bcum[h])).astype(BF16) for h in heads]
        k_in = [(kh[h] * jnp.exp(-bcum[h])).astype(BF16) for h in heads]
        k_out = [(kh[h] * jnp.exp(blast[h] - bcum[h])).astype(BF16) for h in heads]
        vb = [v_refs[h // half][0, rows, dvh[h]].astype(BF16) for h in heads]
        a_intra = [jnp.where(causal, _dot_nt(q_in[h], k_in[h]), 0.0).astype(BF16) for h in heads]
        state = [state_ref[h] for h in heads]
        o = [_dot(a_intra[h], vb[h]) + _dot_nt(q_in[h], state[h].astype(BF16)) for h in heads]
        upd = [_dot_tn(vb[h], k_out[h]) for h in heads]
        for h in heads:
            state_ref[h] = state[h] * jnp.exp(blast[h]) + upd[h]
            y = o[h] * lax.rsqrt(jnp.mean(o[h] * o[h], axis=-1, keepdims=True) + 1e-6) * ng_ref[...]
            o_ref[0, rows, dv[h]] = (y * _silu(r_refs[h // half][0, rows, dvh[h]])).astype(o_ref.dtype)


def _gla(proj, proj_small, a_block, w_a2, b_a, norm_g, q_col):
    B, S, _ = proj.shape
    H, DK, DV, TC = GLA_HEADS, GLA_DK, GLA_DV, GLA_TC
    hv = H * DV // 2
    qb = q_col // (H * DK)
    vb = (q_col + 2 * H * DK) // hv
    assert q_col % (H * DK) == 0 and (q_col + 2 * H * DK) % hv == 0
    return pl.pallas_call(
        _gla_kernel,
        out_shape=jax.ShapeDtypeStruct((B, S, H * DV), BF16),
        grid=(B, S // TC),
        in_specs=[pl.BlockSpec((1, TC, H * DK), lambda b, c: (b, c, qb)),
                  pl.BlockSpec((1, TC, H * DK), lambda b, c: (b, c, qb + 1)),
                  pl.BlockSpec((1, TC, hv), lambda b, c: (b, c, vb)),
                  pl.BlockSpec((1, TC, hv), lambda b, c: (b, c, vb + 1)),
                  pl.BlockSpec((1, TC, hv), lambda b, c: (b, c, vb + 2)),
                  pl.BlockSpec((1, TC, hv), lambda b, c: (b, c, vb + 3)),
                  pl.BlockSpec((1, TC, LANES), lambda b, c: (b, c, a_block)),
                  pl.BlockSpec((GLA_RANK, H * DK), lambda b, c: (0, 0)),
                  pl.BlockSpec((1, H * DK), lambda b, c: (0, 0)),
                  pl.BlockSpec((1, DV), lambda b, c: (0, 0))],
        out_specs=pl.BlockSpec((1, TC, H * DV), lambda b, c: (b, c, 0)),
        scratch_shapes=[pltpu.VMEM((H, DV, DK), F32)],
        compiler_params=_cp("parallel", "arbitrary"),
        name="gla",
    )(proj, proj, proj, proj, proj, proj, proj_small, w_a2, b_a.reshape(1, H * DK), norm_g.reshape(1, DV))


def _conv_ln_kernel(u_ref, halo_ref, w_ref, b_ref, g_ref, beta_ref, o_ref, cat_ref, y_ref):
    TS, HALO, CH = CONV_TS, CONV_HALO, CONV_LANE_CHUNK
    D = u_ref.shape[2]
    n_chunks = D // CH
    first = HALO - (CONV_WIDTH - 1)
    ext = TS + SUBLANES

    @pl.when(pl.program_id(1) == 0)
    def _():
        cat_ref[0:HALO, :] = jnp.zeros((HALO, D), F32)

    @pl.when(pl.program_id(1) > 0)
    def _():
        cat_ref[0:HALO, :] = halo_ref[0]

    cat_ref[HALO:HALO + TS, :] = u_ref[0]
    cat_ref[HALO + TS:HALO + ext, :] = jnp.zeros((SUBLANES, D), F32)

    def conv_chunk(c, total):
        lanes = pl.ds(pl.multiple_of(c * CH, CH), CH)
        acc = jnp.zeros((TS, CH), F32) + b_ref[:, lanes]
        for r in range(SUBLANES):
            part = None
            for a in range((first + CONV_WIDTH - 1) // SUBLANES + 1):
                k = SUBLANES * a + r - first
                if 0 <= k < CONV_WIDTH:
                    term = cat_ref[pl.ds(SUBLANES * a, ext), lanes] * w_ref[pl.ds(k, 1), lanes]
                    part = term if part is None else part + term
            acc = acc + part[r:r + TS]
        y_ref[:, lanes] = acc
        return total + acc

    total = lax.fori_loop(0, n_chunks, conv_chunk, jnp.zeros((TS, CH), F32))
    mu = jnp.broadcast_to(jnp.sum(total, axis=-1, keepdims=True) / D, (TS, CH))

    def var_chunk(c, sq):
        d = y_ref[:, pl.ds(pl.multiple_of(c * CH, CH), CH)] - mu
        return sq + d * d

    sq = lax.fori_loop(0, n_chunks, var_chunk, jnp.zeros((TS, CH), F32), unroll=2)
    inv = jnp.broadcast_to(lax.rsqrt(jnp.sum(sq, axis=-1, keepdims=True) / D + 1e-5), (TS, CH))

    def out_chunk(c, carry):
        lanes = pl.ds(pl.multiple_of(c * CH, CH), CH)
        z = (y_ref[:, lanes] - mu) * inv * g_ref[:, lanes] + beta_ref[:, lanes]
        o_ref[0, :, lanes] = _silu(z).astype(o_ref.dtype)
        return carry

    lax.fori_loop(0, n_chunks, out_chunk, 0, unroll=2)


def _conv_ln_silu(u, w_dw, b_dw, ln_g, ln_b):
    B, S, D = u.shape
    TS, HALO = CONV_TS, CONV_HALO
    ratio = TS // HALO
    vec = pl.BlockSpec((1, D), lambda b, s: (0, 0))
    return pl.pallas_call(
        _conv_ln_kernel,
        out_shape=jax.ShapeDtypeStruct((B, S, D), BF16),
        grid=(B, S // TS),
        in_specs=[pl.BlockSpec((1, TS, D), lambda b, s: (b, s, 0)),
                  pl.BlockSpec((1, HALO, D), lambda b, s: (b, jnp.maximum(s * ratio - 1, 0), 0)),
                  pl.BlockSpec((CONV_WIDTH, D), lambda b, s: (0, 0)),
                  vec, vec, vec],
        out_specs=pl.BlockSpec((1, TS, D), lambda b, s: (b, s, 0)),
        scratch_shapes=[pltpu.VMEM((HALO + TS + SUBLANES, D), F32), pltpu.VMEM((TS, D), F32)],
        compiler_params=_cp("parallel", "parallel"),
        name="conv_ln_silu",
    )(u, u, w_dw, b_dw.reshape(1, D), ln_g.reshape(1, D), ln_b.reshape(1, D))


def _hybrid_attention(h, x2, gate, j, w_in, w_out, q_norm_g, k_norm_g, cmp_pos, cmp_w1, cmp_w2,
                      gla_w_a2, gla_b_a, gla_norm_g, B, S):
    D = h.shape[1]
    n_gate = NSA_HEADS * 3
    o_gl = NSA_Q + 6 * NSA_KV
    o_gq = o_gl + n_gate
    o_ga = o_gq + 2 * GLA_HEADS * GLA_DK + GLA_HEADS * GLA_DV
    o_gr = o_ga + GLA_RANK
    w_main, w_small = _regroup_w_in(w_in, j, ((0, o_gl), (o_gq, o_ga), (o_gr, w_in.shape[2])), (o_gl, o_ga), 256)
    proj = _matmul(h, w_main, F32, 1024, 512).reshape(B, S, -1)
    proj_small = _matmul(h, w_small, F32, 1024, 2 * LANES).reshape(B, S, 2 * LANES)

    t = jnp.arange(S, dtype=jnp.int32)
    cmp_end = jnp.arange(S // CMP_STRIDE, dtype=jnp.int32) * CMP_STRIDE + (CMP_BLOCK - 1)
    qn, ksn, vs_t, kwn, vw_t = _nsa_prep(proj, _rope_tables(t), q_norm_g, k_norm_g)
    kcmp, vcmp_t = _nsa_compress(proj, cmp_pos, cmp_w1, cmp_w2, k_norm_g, _rope_tables(cmp_end))
    gate_logits_t = proj_small[:, :, :n_gate].reshape(B, S, NSA_KV_GROUPS, 3 * NSA_HPG).transpose(0, 2, 3, 1)
    o_nsa = _nsa_attention(qn, kcmp, vcmp_t, ksn, vs_t, kwn, vw_t, gate_logits_t)
    o_gla = _gla(proj, proj_small, 1, gla_w_a2, gla_b_a, gla_norm_g, o_gl)
    xs = (o_nsa.reshape(B * S, -1), o_gla.reshape(B * S, -1))
    return _matmul_residual(xs, w_out, j, jnp.zeros((1, D), F32), x2, gate, S, 1024, 512)


def _conformer(h, x2, gate, j, w_pw1, b_pw1, w_dw, b_dw, ln_g, ln_b, w_pw2, b_pw2, B, S):
    D = h.shape[1]
    tn = 256
    b_pair = b_pw1.reshape(2, D // tn, tn).transpose(1, 0, 2).reshape(1, 2 * D)
    u = _matmul_glu_pair(h, _pair_cast(w_pw1, j, tn, 128), b_pair, F32, 1024, tn)
    v = _conv_ln_silu(u.reshape(B, S, D), w_dw, b_dw, ln_g, ln_b).reshape(B * S, D)
    return _matmul_residual((v,), w_pw2, j, b_pw2.reshape(1, D), x2, gate, S, 1024, 512)


def _swiglu(h, x2, gate, layer, w_gate, w_up, w_down, S):
    D = h.shape[1]
    act = _matmul_swiglu(h, w_gate, w_up, layer, BF16, 1024, 256)
    return _matmul_residual((act,), _cast_bf16(w_down, layer, 256), None, jnp.zeros((1, D), F32), x2, gate, S,
                            1024, 256)


def kernel(x, c, w_mod, b_mod, ada_table, norm_mix_g, norm_ffn_g, w_in, w_out, q_norm_g, k_norm_g, cmp_pos, cmp_w1, cmp_w2, gla_w_a2, gla_b_a, gla_norm_g, cv_w_pw1, cv_b_pw1, cv_w_dw, cv_b_dw, cv_ln_g, cv_ln_b, cv_w_pw2, cv_b_pw2, ffn_w_gate, ffn_w_up, ffn_w_down):
    B, S, D = x.shape
    depth = ada_table.shape[0]
    mod = _ada_mod(c, w_mod, b_mod).reshape(B, N_MOD, D)
    for layer in range(depth):
        m = mod + ada_table[layer]
        sh_a, sc_a, g_a, sh_f, sc_f, g_f = [m[:, i, :] for i in range(N_MOD)]
        h = _norm_mod(x, norm_mix_g[layer], sc_a, sh_a).reshape(B * S, D)
        x2 = x.reshape(B * S, D)
        g_a3 = g_a.reshape(B, 1, D)
        j = layer // 2
        if layer % 2 == 0:
            x2 = _hybrid_attention(h, x2, g_a3, j, w_in, w_out, q_norm_g[j], k_norm_g[j], cmp_pos[j],
                                   cmp_w1[j], cmp_w2[j], gla_w_a2[j], gla_b_a[j], gla_norm_g[j], B, S)
        else:
            x2 = _conformer(h, x2, g_a3, j, cv_w_pw1, cv_b_pw1[j], cv_w_dw[j], cv_b_dw[j], cv_ln_g[j],
                            cv_ln_b[j], cv_w_pw2, cv_b_pw2[j], B, S)
        x = x2.reshape(B, S, D)
        h = _norm_mod(x, norm_ffn_g[layer], sc_f, sh_f).reshape(B * S, D)
        x2 = _swiglu(h, x2, g_f.reshape(B, 1, D), layer, ffn_w_gate, ffn_w_up, ffn_w_down, S)
        x = x2.reshape(B, S, D)
    return x
```

```python
import functools

import numpy as np
import jax
import jax.numpy as jnp
from jax import lax
from jax.experimental import pallas as pl
from jax.experimental.pallas import tpu as pltpu

F32 = jnp.float32
BF16 = jnp.bfloat16

HEAD_DIM = 128
NSA_HEADS = 16
NSA_KV_GROUPS = 4
NSA_HPG = NSA_HEADS // NSA_KV_GROUPS
CMP_BLOCK = 32
CMP_STRIDE = 16
SLC_BLOCK = 64
SLC_TOPK = 16
WINDOW = 512
ROPE_THETA = 500000.0
ROPE_DIM = HEAD_DIM // 4
GLA_HEADS = 4
GLA_DK = 256
GLA_DV = 512
GLA_RANK = 16
GLA_TAU = 16.0
GLA_CHUNK = 64
CONV_WIDTH = 31
N_MOD = 6
NSA_Q = NSA_HEADS * HEAD_DIM
NSA_KV = NSA_KV_GROUPS * HEAD_DIM

VMEM_LIMIT_BYTES = 56 * 1024 * 1024
LANES = 128
SUBLANES = 8
NEG_BIG = -1e30
M_INIT = -1e29

ATTN_TQ = 128
ATTN_UNROLL = 4
CONV_TS = 256
CONV_HALO = 32
CONV_LANE_CHUNK = 128
GLA_TC = 128


def _cp(*sem):
    return pltpu.CompilerParams(dimension_semantics=sem, vmem_limit_bytes=VMEM_LIMIT_BYTES)


def _dot(a, b):
    return jnp.dot(a, b, preferred_element_type=F32)


def _dot_nt(a, b):
    return lax.dot_general(a, b, (((1,), (1,)), ((), ())), preferred_element_type=F32)


def _dot_tn(a, b):
    return lax.dot_general(a, b, (((0,), (0,)), ((), ())), preferred_element_type=F32)


def _silu(x):
    return x * jax.nn.sigmoid(x)


def _cast_weights_once(w_refs, wbf_refs):
    @pl.when(pl.program_id(1) == 0)
    def _():
        for w_ref, wbf_ref in zip(w_refs, wbf_refs):
            wbf_ref[...] = w_ref[...].astype(BF16)


def _mm_kernel(x_ref, w_ref, o_ref):
    o_ref[...] = _dot(x_ref[...], w_ref[...]).astype(o_ref.dtype)


def _matmul(x, w, out_dtype, tm, tn):
    M, K = x.shape
    N = w.shape[1]
    return pl.pallas_call(
        _mm_kernel,
        out_shape=jax.ShapeDtypeStruct((M, N), out_dtype),
        grid=(M // tm, N // tn),
        in_specs=[pl.BlockSpec((tm, K), lambda i, j: (i, 0)),
                  pl.BlockSpec((K, tn), lambda i, j: (0, j))],
        out_specs=pl.BlockSpec((tm, tn), lambda i, j: (i, j)),
        compiler_params=_cp("parallel", "parallel"),
        name="matmul",
    )(x, w)


def _mm_swiglu_kernel(x_ref, wg_ref, wu_ref, o_ref, wgbf_ref, wubf_ref):
    _cast_weights_once((wg_ref, wu_ref), (wgbf_ref, wubf_ref))
    x = x_ref[...]
    o_ref[...] = (_silu(_dot(x, wgbf_ref[...])) * _dot(x, wubf_ref[...])).astype(o_ref.dtype)


def _matmul_swiglu(x, w_gate, w_up, layer, out_dtype, tm, tn):
    M, K = x.shape
    N = w_gate.shape[2]
    w_spec = pl.BlockSpec((None, K, tn), lambda j, i: (layer, 0, j))
    return pl.pallas_call(
        _mm_swiglu_kernel,
        out_shape=jax.ShapeDtypeStruct((M, N), out_dtype),
        grid=(N // tn, M // tm),
        in_specs=[pl.BlockSpec((tm, K), lambda j, i: (i, 0)), w_spec, w_spec],
        out_specs=pl.BlockSpec((tm, tn), lambda j, i: (i, j)),
        scratch_shapes=[pltpu.VMEM((K, tn), BF16), pltpu.VMEM((K, tn), BF16)],
        compiler_params=_cp("arbitrary", "arbitrary"),
        name="matmul_swiglu",
    )(x, w_gate, w_up)


def _mm_glu_pair_kernel(x_ref, w_ref, b_ref, o_ref):
    tn = o_ref.shape[1]
    x = x_ref[...]
    a = _dot(x, w_ref[:, :tn]) + b_ref[:, :tn]
    b = _dot(x, w_ref[:, tn:]) + b_ref[:, tn:]
    o_ref[...] = (a * jax.nn.sigmoid(b)).astype(o_ref.dtype)


def _matmul_glu_pair(x, w_pair, b_pair, out_dtype, tm, tn):
    M, K = x.shape
    N = w_pair.shape[1] // 2
    return pl.pallas_call(
        _mm_glu_pair_kernel,
        out_shape=jax.ShapeDtypeStruct((M, N), out_dtype),
        grid=(M // tm, N // tn),
        in_specs=[pl.BlockSpec((tm, K), lambda i, j: (i, 0)),
                  pl.BlockSpec((K, 2 * tn), lambda i, j: (0, j)),
                  pl.BlockSpec((1, 2 * tn), lambda i, j: (0, j))],
        out_specs=pl.BlockSpec((tm, tn), lambda i, j: (i, j)),
        compiler_params=_cp("parallel", "parallel"),
        name="matmul_glu_pair",
    )(x, w_pair, b_pair)


def _pair_cast_kernel(w_ref, o_ref, *, tn):
    half = w_ref.shape[1] // 2
    for j in range(half // tn):
        o_ref[:, 2 * j * tn:(2 * j + 1) * tn] = w_ref[:, j * tn:(j + 1) * tn].astype(BF16)
        o_ref[:, (2 * j + 1) * tn:(2 * j + 2) * tn] = w_ref[:, half + j * tn:half + (j + 1) * tn].astype(BF16)


def _pair_cast(w, layer, tn, rows):
    _, K, N2 = w.shape
    return pl.pallas_call(
        functools.partial(_pair_cast_kernel, tn=tn),
        out_shape=jax.ShapeDtypeStruct((K, N2), BF16),
        grid=(K // rows,),
        in_specs=[pl.BlockSpec((None, rows, N2), lambda i: (layer, i, 0))],
        out_specs=pl.BlockSpec((rows, N2), lambda i: (i, 0)),
        compiler_params=_cp("parallel"),
        name="pair_cast",
    )(w)


def _mm_res_kernel(*refs, n_x, cast):
    x_refs = refs[:n_x]
    if cast:
        w_ref, b_ref, res_ref, g_ref, o_ref, wbf_ref = refs[n_x:]
        _cast_weights_once((w_ref,), (wbf_ref,))
    else:
        wbf_ref, b_ref, res_ref, g_ref, o_ref = refs[n_x:]
    y = b_ref[...]
    off = 0
    for x_ref in x_refs:
        k = x_ref.shape[1]
        y = y + _dot(x_ref[...], wbf_ref[off:off + k, :])
        off += k
    o_ref[...] = res_ref[...] + g_ref[0] * y


def _matmul_residual(xs, w, layer, bias, res, gate, rows_per_batch, tm, tn):
    M = xs[0].shape[0]
    K, N = w.shape[-2:]
    bpt = rows_per_batch // tm
    cast = w.ndim == 3
    w_block = (None, K, tn) if cast else (K, tn)
    w_index = (lambda i, j: (layer, 0, j)) if cast else (lambda i, j: (0, j))
    if cast:
        grid = (N // tn, M // tm)
        ij = lambda f: (lambda j, i: f(i, j))
        x_mode, w_mode = None, pl.Buffered(1)
        scratch = [pltpu.VMEM((K, tn), BF16)]
    else:
        grid = (M // tm, N // tn)
        ij = lambda f: f
        x_mode, w_mode = pl.Buffered(1), None
        scratch = []
    return pl.pallas_call(
        functools.partial(_mm_res_kernel, n_x=len(xs), cast=cast),
        out_shape=jax.ShapeDtypeStruct((M, N), F32),
        grid=grid,
        in_specs=[pl.BlockSpec((tm, x.shape[1]), ij(lambda i, j: (i, 0)), pipeline_mode=x_mode) for x in xs] + [
                  pl.BlockSpec(w_block, ij(w_index), pipeline_mode=w_mode),
                  pl.BlockSpec((1, tn), ij(lambda i, j: (0, j))),
                  pl.BlockSpec((tm, tn), ij(lambda i, j: (i, j))),
                  pl.BlockSpec((1, 1, tn), ij(lambda i, j: (i // bpt, 0, j)))],
        out_specs=pl.BlockSpec((tm, tn), ij(lambda i, j: (i, j))),
        scratch_shapes=scratch,
        compiler_params=_cp("arbitrary", "arbitrary"),
        name="matmul_residual",
    )(*xs, w, bias, res, gate)


def _cast_kernel(w_ref, o_ref):
    o_ref[...] = w_ref[...].astype(o_ref.dtype)


def _cast_bf16(w, layer, rows):
    _, K, N = w.shape
    return pl.pallas_call(
        _cast_kernel,
        out_shape=jax.ShapeDtypeStruct((K, N), BF16),
        grid=(K // rows,),
        in_specs=[pl.BlockSpec((None, rows, N), lambda i: (layer, i, 0))],
        out_specs=pl.BlockSpec((rows, N), lambda i: (i, 0)),
        compiler_params=_cp("parallel"),
        name="cast_bf16",
    )(w)


def _regroup_kernel(wt_ref, main_ref, small_ref, *, segments, small_starts):
    off = 0
    for lo, hi in segments:
        for r in range(lo, hi, LANES):
            main_ref[:, off:off + LANES] = wt_ref[r:r + LANES, :].T.astype(BF16)
            off += LANES
    for n, lo in enumerate(small_starts):
        small_ref[:, n * LANES:(n + 1) * LANES] = wt_ref[lo:lo + LANES, :].T.astype(BF16)


def _regroup_w_in(w_in, layer, segments, small_starts, tk):
    wt = jnp.swapaxes(w_in, 1, 2)
    _, N, K = wt.shape
    assert all((hi - lo) % LANES == 0 and lo % SUBLANES == 0 for lo, hi in segments)
    assert all(lo % SUBLANES == 0 and lo + LANES <= N for lo in small_starts)
    n_main = sum(hi - lo for lo, hi in segments)
    n_small = LANES * len(small_starts)
    return pl.pallas_call(
        functools.partial(_regroup_kernel, segments=segments, small_starts=small_starts),
        out_shape=(jax.ShapeDtypeStruct((K, n_main), BF16), jax.ShapeDtypeStruct((K, n_small), BF16)),
        grid=(K // tk,),
        in_specs=[pl.BlockSpec((None, N, tk), lambda k: (layer, 0, k))],
        out_specs=(pl.BlockSpec((tk, n_main), lambda k: (k, 0)), pl.BlockSpec((tk, n_small), lambda k: (k, 0))),
        compiler_params=_cp("parallel"),
        name="regroup_w_in",
    )(wt)


def _mod_kernel(c_ref, w_ref, b_ref, o_ref):
    a = _silu(c_ref[...]).astype(BF16)
    o_ref[...] = _dot(a, w_ref[...].astype(BF16)) + b_ref[...]


def _ada_mod(c, w_mod, b_mod):
    B, D = c.shape
    N = w_mod.shape[1]
    tn = 512
    return pl.pallas_call(
        _mod_kernel,
        out_shape=jax.ShapeDtypeStruct((B, N), F32),
        grid=(N // tn,),
        in_specs=[pl.BlockSpec((B, D), lambda j: (0, 0)),
                  pl.BlockSpec((D, tn), lambda j: (0, j)),
                  pl.BlockSpec((1, tn), lambda j: (0, j))],
        out_specs=pl.BlockSpec((B, tn), lambda j: (0, j)),
        compiler_params=_cp("parallel"),
        name="ada_mod",
    )(c, w_mod, b_mod.reshape(1, N))


def _norm_mod_kernel(x_ref, g_ref, sc_ref, sh_ref, o_ref, gm_ref):
    rows_per_trip = 2 * SUBLANES
    gm_ref[...] = g_ref[...] * (1.0 + sc_ref[0])

    def trip(i, carry):
        rows = pl.ds(pl.multiple_of(i * rows_per_trip, rows_per_trip), rows_per_trip)
        x = x_ref[0, rows, :]
        r = lax.rsqrt(jnp.mean(x * x, axis=-1, keepdims=True) + 1e-6)
        o_ref[0, rows, :] = (x * r * gm_ref[...] + sh_ref[0]).astype(o_ref.dtype)
        return carry

    lax.fori_loop(0, x_ref.shape[1] // rows_per_trip, trip, 0, unroll=4)


def _norm_mod(x, g, sc, sh):
    B, S, D = x.shape
    ts = 512
    return pl.pallas_call(
        _norm_mod_kernel,
        out_shape=jax.ShapeDtypeStruct((B, S, D), BF16),
        grid=(B, S // ts),
        in_specs=[pl.BlockSpec((1, ts, D), lambda b, s: (b, s, 0)),
                  pl.BlockSpec((1, D), lambda b, s: (0, 0)),
                  pl.BlockSpec((1, 1, D), lambda b, s: (b, 0, 0)),
                  pl.BlockSpec((1, 1, D), lambda b, s: (b, 0, 0))],
        out_specs=pl.BlockSpec((1, ts, D), lambda b, s: (b, s, 0)),
        scratch_shapes=[pltpu.VMEM((1, D), F32)],
        compiler_params=_cp("parallel", "parallel"),
        name="norm_mod",
    )(x, g.reshape(1, D), sc.reshape(B, 1, D), sh.reshape(B, 1, D))


def _rope_tables(pos):
    half = ROPE_DIM // 2
    inv_freq = ROPE_THETA ** (-jnp.arange(half, dtype=F32) / half)
    ang = pos.astype(F32)[:, None] * inv_freq[None, :]
    cos, sin = jnp.cos(ang), jnp.sin(ang)
    n = pos.shape[0]
    rest = HEAD_DIM - ROPE_DIM
    c = jnp.concatenate([cos, cos, jnp.ones((n, rest), F32)], axis=-1)
    s_lo = jnp.concatenate([-sin, jnp.zeros((n, HEAD_DIM - half), F32)], axis=-1)
    s_hi = jnp.concatenate([jnp.zeros((n, half), F32), sin, jnp.zeros((n, rest), F32)], axis=-1)
    return c, s_lo, s_hi


def _norm_rope(x, g, c, s_lo, s_hi):
    y = x * lax.rsqrt(jnp.mean(x * x, axis=-1, keepdims=True) + 1e-6) * g
    half = ROPE_DIM // 2
    return y * c + pltpu.roll(y, HEAD_DIM - half, 1) * s_lo + pltpu.roll(y, half, 1) * s_hi


def _nsa_prep_kernel(q_ref, ks_ref, vs_ref, kw_ref, vw_ref, c_ref, slo_ref, shi_ref, qg_ref, kg_ref,
                     qo_ref, kso_ref, vso_ref, kwo_ref, vwo_ref):
    c, s_lo, s_hi = c_ref[...], slo_ref[...], shi_ref[...]
    scale = HEAD_DIM ** -0.5
    for h in range(NSA_HEADS):
        sl = slice(h * HEAD_DIM, (h + 1) * HEAD_DIM)
        qo_ref[0, :, sl] = (_norm_rope(q_ref[0, :, sl], qg_ref[...], c, s_lo, s_hi) * scale).astype(BF16)
    for g in range(NSA_KV_GROUPS):
        sl = slice(g * HEAD_DIM, (g + 1) * HEAD_DIM)
        kso_ref[0, :, sl] = _norm_rope(ks_ref[0, :, sl], kg_ref[1:2, :], c, s_lo, s_hi).astype(BF16)
        kwo_ref[0, :, sl] = _norm_rope(kw_ref[0, :, sl], kg_ref[2:3, :], c, s_lo, s_hi).astype(BF16)
    vso_ref[0] = vs_ref[0].T.astype(BF16)
    vwo_ref[0] = vw_ref[0].T.astype(BF16)


def _nsa_prep(proj, tables, q_norm_g, k_norm_g):
    B, S, _ = proj.shape
    ts = 256
    kvb = NSA_Q // NSA_KV

    def kv_spec(n):
        return pl.BlockSpec((1, ts, NSA_KV), lambda b, s: (b, s, kvb + n))

    tab = pl.BlockSpec((ts, HEAD_DIM), lambda b, s: (s, 0))
    out_k = pl.BlockSpec((1, ts, NSA_KV), lambda b, s: (b, s, 0))
    out_vt = pl.BlockSpec((1, NSA_KV, ts), lambda b, s: (b, 0, s))
    k_shape = jax.ShapeDtypeStruct((B, S, NSA_KV), BF16)
    vt_shape = jax.ShapeDtypeStruct((B, NSA_KV, S), BF16)
    return pl.pallas_call(
        _nsa_prep_kernel,
        out_shape=(jax.ShapeDtypeStruct((B, S, NSA_Q), BF16), k_shape, vt_shape, k_shape, vt_shape),
        grid=(B, S // ts),
        in_specs=[pl.BlockSpec((1, ts, NSA_Q), lambda b, s: (b, s, 0)),
                  kv_spec(2), kv_spec(3), kv_spec(4), kv_spec(5), tab, tab, tab,
                  pl.BlockSpec((1, HEAD_DIM), lambda b, s: (0, 0)),
                  pl.BlockSpec((3, HEAD_DIM), lambda b, s: (0, 0))],
        out_specs=(pl.BlockSpec((1, ts, NSA_Q), lambda b, s: (b, s, 0)), out_k, out_vt, out_k, out_vt),
        compiler_params=_cp("parallel", "parallel"),
        name="nsa_prep",
    )(proj, proj, proj, proj, proj, *tables, q_norm_g.reshape(1, HEAD_DIM), k_norm_g)


def _compress_kernel(kc_ref, vc_ref, pos_ref, w1_ref, w2_ref, kg_ref, c_ref, slo_ref, shi_ref,
                     ko_ref, vo_ref):
    half_blk = CMP_BLOCK // 2
    n_seg = kc_ref.shape[1] // CMP_STRIDE

    def compress(tok_ref, j):
        u = jnp.zeros((n_seg, w1_ref.shape[2]), F32)
        v = jnp.zeros((n_seg, w1_ref.shape[2]), F32)
        for l in range(half_blk):
            x = tok_ref[0, pl.ds(l, n_seg, stride=CMP_STRIDE), :]
            xa = (x + pos_ref[j, l:l + 1, :]).astype(BF16)
            xb = (x + pos_ref[j, half_blk + l:half_blk + l + 1, :]).astype(BF16)
            u = u + _dot(xa, w1_ref[j, l * HEAD_DIM:(l + 1) * HEAD_DIM, :])
            v = v + _dot(xb, w1_ref[j, (half_blk + l) * HEAD_DIM:(half_blk + l + 1) * HEAD_DIM, :])
        h = u + pltpu.roll(v, n_seg - 1, 0)
        return _dot(jax.nn.gelu(h).astype(BF16), w2_ref[j])

    k = compress(kc_ref, 0)
    ko_ref[0, 0] = _norm_rope(k, kg_ref[0:1, :], c_ref[...], slo_ref[...], shi_ref[...]).astype(BF16)
    vo_ref[0, 0] = compress(vc_ref, 1).T.astype(BF16)


def _nsa_compress(proj, cmp_pos, cmp_w1, cmp_w2, k_norm_g, cmp_tables):
    B, S, _ = proj.shape
    G = NSA_KV_GROUPS
    n_seg = S // CMP_STRIDE
    kcb = NSA_Q // HEAD_DIM
    full2 = lambda b, g: (0, 0)
    full3 = lambda b, g: (0, 0, 0)
    return pl.pallas_call(
        _compress_kernel,
        out_shape=(jax.ShapeDtypeStruct((B, G, n_seg, HEAD_DIM), BF16),
                   jax.ShapeDtypeStruct((B, G, HEAD_DIM, n_seg), BF16)),
        grid=(B, G),
        in_specs=[pl.BlockSpec((1, S, HEAD_DIM), lambda b, g: (b, 0, kcb + g)),
                  pl.BlockSpec((1, S, HEAD_DIM), lambda b, g: (b, 0, kcb + G + g)),
                  pl.BlockSpec(cmp_pos.shape, full3),
                  pl.BlockSpec(cmp_w1.shape, full3),
                  pl.BlockSpec(cmp_w2.shape, full3),
                  pl.BlockSpec((3, HEAD_DIM), full2),
                  pl.BlockSpec((n_seg, HEAD_DIM), full2),
                  pl.BlockSpec((n_seg, HEAD_DIM), full2),
                  pl.BlockSpec((n_seg, HEAD_DIM), full2)],
        out_specs=(pl.BlockSpec((1, 1, n_seg, HEAD_DIM), lambda b, g: (b, g, 0, 0)),
                   pl.BlockSpec((1, 1, HEAD_DIM, n_seg), lambda b, g: (b, g, 0, 0))),
        compiler_params=_cp("parallel", "parallel"),
        name="nsa_compress",
    )(proj, proj, cmp_pos, cmp_w1.astype(BF16), cmp_w2.astype(BF16), k_norm_g, *cmp_tables)


def _fold_rows(x, op):
    return op(x.reshape(x.shape[0] // SUBLANES, SUBLANES, x.shape[1]), axis=0)


def _nsa_attn_kernel(q_ref, kc_ref, vct_ref, ks_ref, vst_ref, kw_ref, vwt_ref, gate_ref, ovl_ref, exp_ref,
                     o_ref, selm_ref, ss_ref, sw_ref):
    TQ, TK = ATTN_TQ, LANES
    R = NSA_HPG * TQ
    qi = pl.program_id(2)
    t0 = qi * TQ
    q = jnp.concatenate([q_ref[0, :, h * HEAD_DIM:(h + 1) * HEAD_DIM] for h in range(NSA_HPG)], axis=0)
    tq = t0 + (lax.broadcasted_iota(jnp.int32, (TK, R), 1) & (TQ - 1))
    key = lax.broadcasted_iota(jnp.int32, (TK, R), 0)

    def score_tile(k_ref, scr_ref, slot, kt, m8, use_sel=False, causal=False, window=False, valid=None):
        off = pl.multiple_of(kt * TK, TK)
        s = _dot_nt(k_ref[0, pl.ds(off, TK), :], q)
        mask = None
        if use_sel:
            sm = selm_ref[pl.ds(off, TK), :]
            mask = jnp.concatenate([sm] * NSA_HPG, axis=1) > 0.5
        if causal:
            c = (off + key) <= tq
            mask = c if mask is None else mask & c
        if window:
            w = (tq - (off + key)) < WINDOW
            mask = w if mask is None else mask & w
        if valid is not None:
            mask = valid if mask is None else mask & valid
        if mask is not None:
            s = jnp.where(mask, s, NEG_BIG)
        scr_ref[slot] = s
        return jnp.maximum(m8, _fold_rows(s, jnp.max))

    def value_tile(scr_ref, vt_ref, slot, kt, m, carry):
        l8, acc = carry
        off = pl.multiple_of(kt * TK, TK)
        p = jnp.exp(scr_ref[slot] - m)
        return l8 + _fold_rows(p, jnp.sum), acc + _dot(vt_ref[0, :, pl.ds(off, TK)], p.astype(BF16))

    def finish(carry):
        l8, acc = carry
        l = jnp.sum(l8, axis=0, keepdims=True)
        return acc / jnp.where(l > 0.0, l, 1.0)

    m_init = jnp.full((SUBLANES, R), M_INIT, F32)
    acc_init = (jnp.zeros((SUBLANES, R), F32), jnp.zeros((HEAD_DIM, R), F32))

    n_back = WINDOW // TK
    win_tiles = [(jnp.maximum(qi - n_back + u, 0), qi - n_back + u >= 0) for u in range(n_back)]
    m8_w = m_init
    for u, (kt, valid) in enumerate(win_tiles):
        m8_w = score_tile(kw_ref, sw_ref, u, kt, m8_w, window=(u == 0), valid=valid)
    m8_w = score_tile(kw_ref, sw_ref, n_back, qi, m8_w, causal=True)
    m_w = jnp.max(m8_w, axis=0, keepdims=True)

    s = _dot_nt(kc_ref[0, 0], q)
    mask = (key * CMP_STRIDE + (CMP_BLOCK - 1)) <= tq
    s = jnp.where(mask, s, NEG_BIG)
    p = jnp.where(mask, jnp.exp(s - jnp.max(s, axis=0, keepdims=True)), 0.0)
    l = jnp.sum(p, axis=0, keepdims=True)
    pb = (p / jnp.where(l > 0.0, l, 1.0)).astype(BF16)
    o_cmp = _dot(vct_ref[0, 0], pb)

    c_w = acc_init
    for u, (kt, _) in enumerate(win_tiles):
        c_w = value_tile(sw_ref, vwt_ref, u, kt, m_w, c_w)
    o_win = finish(value_tile(sw_ref, vwt_ref, n_back, qi, m_w, c_w))

    n_slc = ovl_ref.shape[0]
    ranked = t0 + TQ > SLC_TOPK * SLC_BLOCK
    last_group = qi // ATTN_UNROLL
    group_keys = ATTN_UNROLL * TK

    @pl.when(ranked)
    def _():
        imp_heads = _dot(ovl_ref[...], pb)
        imp = imp_heads[:, 0:TQ]
        for h in range(1, NSA_HPG):
            imp = imp + imp_heads[:, h * TQ:(h + 1) * TQ]
        t = t0 + lax.broadcasted_iota(jnp.int32, (n_slc, TQ), 1)
        blk = lax.broadcasted_iota(jnp.int32, (n_slc, TQ), 0)
        cur = t // SLC_BLOCK
        forced = (blk == 0) | (blk == cur) | (blk == cur - 1)
        valid = blk * SLC_BLOCK <= t
        val = jnp.where(forced, jnp.inf, jnp.where(valid, imp, -jnp.inf))
        rank = jnp.zeros((n_slc, TQ), F32)
        for i in range(n_slc):
            vi = val[i:i + 1, :]
            ahead = (vi > val) | ((vi == val) & (blk > i))
            rank = rank + jnp.where(ahead, 1.0, 0.0)
        sel = jnp.where((rank < float(SLC_TOPK)) & (val > -jnp.inf), 1.0, 0.0).astype(BF16)
        selm_ref[...] = _dot(exp_ref[...], sel)

    @pl.when(jnp.logical_not(ranked))
    def _():
        rows = pl.ds(pl.multiple_of(last_group * group_keys, group_keys), group_keys)
        selm_ref[rows, :] = jnp.ones((group_keys, TQ), F32)

    def score_group(a, m8, use_sel, causal):
        for u in range(ATTN_UNROLL):
            kt = a * ATTN_UNROLL + u
            m8 = score_tile(ks_ref, ss_ref, kt, kt, m8, use_sel=use_sel, causal=causal)
        return m8

    n_masked = jnp.where(ranked, last_group, 0)
    m8 = lax.fori_loop(0, n_masked, lambda a, m: score_group(a, m, True, False), m_init)
    m8 = lax.fori_loop(0, last_group - n_masked, lambda a, m: score_group(a, m, False, False), m8)
    m8_s = score_group(last_group, m8, True, True)
    m_s = jnp.max(m8_s, axis=0, keepdims=True)

    def value_group(a, carry):
        for u in range(ATTN_UNROLL):
            kt = a * ATTN_UNROLL + u
            carry = value_tile(ss_ref, vst_ref, kt, kt, m_s, carry)
        return carry

    o_slc = finish(lax.fori_loop(0, last_group + 1, value_group, acc_init))

    gate = jax.nn.sigmoid(gate_ref[0, 0])
    for h in range(NSA_HPG):
        cols = slice(h * TQ, (h + 1) * TQ)
        o = (gate[3 * h:3 * h + 1, :] * o_cmp[:, cols] + gate[3 * h + 1:3 * h + 2, :] * o_slc[:, cols]
             + gate[3 * h + 2:3 * h + 3, :] * o_win[:, cols])
        o_ref[0, :, h * HEAD_DIM:(h + 1) * HEAD_DIM] = o.T.astype(o_ref.dtype)


def _nsa_attention(qn, kcmp, vcmp_t, ksn, vs_t, kwn, vw_t, gate_logits_t):
    B, S, _ = qn.shape
    G = NSA_KV_GROUPS
    TQ = ATTN_TQ
    n_cmp = S // CMP_STRIDE
    n_slc = S // SLC_BLOCK
    assert n_cmp == LANES and WINDOW % LANES == 0 and TQ == LANES
    cmp_start = np.arange(n_cmp) * CMP_STRIDE
    slc_start = np.arange(n_slc) * SLC_BLOCK
    overlap = np.clip(np.minimum(cmp_start[None, :] + CMP_BLOCK, slc_start[:, None] + SLC_BLOCK)
                      - np.maximum(cmp_start[None, :], slc_start[:, None]), 0, None) / CMP_STRIDE
    ovl = jnp.asarray(overlap, dtype=BF16)
    expand = jnp.asarray((np.arange(S)[:, None] // SLC_BLOCK) == np.arange(n_slc)[None, :], dtype=BF16)
    gq = NSA_HPG * HEAD_DIM
    q_spec = pl.BlockSpec((1, TQ, gq), lambda b, g, i: (b, i, g))
    k_spec = pl.BlockSpec((1, S, HEAD_DIM), lambda b, g, i: (b, 0, g))
    vt_spec = pl.BlockSpec((1, HEAD_DIM, S), lambda b, g, i: (b, g, 0))
    return pl.pallas_call(
        _nsa_attn_kernel,
        out_shape=jax.ShapeDtypeStruct((B, S, NSA_Q), BF16),
        grid=(B, G, S // TQ),
        in_specs=[q_spec,
                  pl.BlockSpec((1, 1, n_cmp, HEAD_DIM), lambda b, g, i: (b, g, 0, 0)),
                  pl.BlockSpec((1, 1, HEAD_DIM, n_cmp), lambda b, g, i: (b, g, 0, 0)),
                  k_spec, vt_spec, k_spec, vt_spec,
                  pl.BlockSpec((1, 1, 3 * NSA_HPG, TQ), lambda b, g, i: (b, g, 0, i)),
                  pl.BlockSpec(ovl.shape, lambda b, g, i: (0, 0)),
                  pl.BlockSpec(expand.shape, lambda b, g, i: (0, 0))],
        out_specs=q_spec,
        scratch_shapes=[pltpu.VMEM((S, TQ), F32),
                        pltpu.VMEM((S // LANES, LANES, NSA_HPG * TQ), F32),
                        pltpu.VMEM((WINDOW // LANES + 1, LANES, NSA_HPG * TQ), F32)],
        compiler_params=_cp("parallel", "parallel", "parallel"),
        name="nsa_attention",
    )(qn, kcmp, vcmp_t, ksn, vs_t, kwn, vw_t, gate_logits_t, ovl, expand)


def _gla_kernel(q_ref, k_ref, v0_ref, v1_ref, r0_ref, r1_ref, a_ref, wa_ref, ba_ref, ng_ref, o_ref, state_ref):
    C, DK, DV = GLA_CHUNK, GLA_DK, GLA_DV
    half = GLA_HEADS // 2
    v_refs, r_refs = (v0_ref, v1_ref), (r0_ref, r1_ref)

    @pl.when(pl.program_id(1) == 0)
    def _():
        state_ref[...] = jnp.zeros_like(state_ref)

    row = lax.broadcasted_iota(jnp.int32, (C, C), 0)
    colc = lax.broadcasted_iota(jnp.int32, (C, C), 1)
    causal = colc <= row
    tri = jnp.where(causal, 1.0, 0.0).astype(BF16)
    wa = wa_ref[...].astype(BF16)
    heads = range(GLA_HEADS)
    for c in range(q_ref.shape[1] // C):
        rows = slice(c * C, (c + 1) * C)
        a_low = a_ref[0, rows, 0:GLA_RANK].astype(BF16)
        dk = [slice(h * DK, (h + 1) * DK) for h in heads]
        dv = [slice(h * DV, (h + 1) * DV) for h in heads]
        dvh = [slice((h % half) * DV, (h % half + 1) * DV) for h in heads]
        z = [_dot(a_low, wa[:, dk[h]]) + ba_ref[:, dk[h]] for h in heads]
        la = [(jnp.minimum(z[h], 0.0) - jnp.log1p(jnp.exp(-jnp.abs(z[h])))) / GLA_TAU for h in heads]
        hi = [la[h].astype(BF16) for h in heads]
        r1 = [la[h] - hi[h].astype(F32) for h in heads]
        mid = [r1[h].astype(BF16) for h in heads]
        lo = [(r1[h] - mid[h].astype(F32)).astype(BF16) for h in heads]
        bcum = [_dot(tri, hi[h]) + _dot(tri, mid[h]) + _dot(tri, lo[h]) for h in heads]
        blast = [bcum[h][C - 1:C, :] for h in heads]
        kh = [k_ref[0, rows, dk[h]] for h in heads]
        q_in = [(q_ref[0, rows, dk[h]] * (DK ** -0.5) * jnp.exp(bcum[h])).astype(BF16) for h in heads]
        k_in = [(kh[h] * jnp.exp(-bcum[h])).astype(BF16) for h in heads]
        k_out = [(kh[h] * jnp.exp(blast[h] - bcum[h])).astype(BF16) for h in heads]
        vb = [v_refs[h // half][0, rows, dvh[h]].astype(BF16) for h in heads]
        a_intra = [jnp.where(causal, _dot_nt(q_in[h], k_in[h]), 0.0).astype(BF16) for h in heads]
        state = [state_ref[h] for h in heads]
        o = [_dot(a_intra[h], vb[h]) + _dot_nt(q_in[h], state[h].astype(BF16)) for h in heads]
        upd = [_dot_tn(vb[h], k_out[h]) for h in heads]
        for h in heads:
            state_ref[h] = state[h] * jnp.exp(blast[h]) + upd[h]
            y = o[h] * lax.rsqrt(jnp.mean(o[h] * o[h], axis=-1, keepdims=True) + 1e-6) * ng_ref[...]
            o_ref[0, rows, dv[h]] = (y * _silu(r_refs[h // half][0, rows, dvh[h]])).astype(o_ref.dtype)


def _gla(proj, proj_small, a_block, w_a2, b_a, norm_g, q_col):
    B, S, _ = proj.shape
    H, DK, DV, TC = GLA_HEADS, GLA_DK, GLA_DV, GLA_TC
    hv = H * DV // 2
    qb = q_col // (H * DK)
    vb = (q_col + 2 * H * DK) // hv
    assert q_col % (H * DK) == 0 and (q_col + 2 * H * DK) % hv == 0
    return pl.pallas_call(
        _gla_kernel,
        out_shape=jax.ShapeDtypeStruct((B, S, H * DV), BF16),
        grid=(B, S // TC),
        in_specs=[pl.BlockSpec((1, TC, H * DK), lambda b, c: (b, c, qb)),
                  pl.BlockSpec((1, TC, H * DK), lambda b, c: (b, c, qb + 1)),
                  pl.BlockSpec((1, TC, hv), lambda b, c: (b, c, vb)),
                  pl.BlockSpec((1, TC, hv), lambda b, c: (b, c, vb + 1)),
                  pl.BlockSpec((1, TC, hv), lambda b, c: (b, c, vb + 2)),
                  pl.BlockSpec((1, TC, hv), lambda b, c: (b, c, vb + 3)),
                  pl.BlockSpec((1, TC, LANES), lambda b, c: (b, c, a_block)),
                  pl.BlockSpec((GLA_RANK, H * DK), lambda b, c: (0, 0)),
                  pl.BlockSpec((1, H * DK), lambda b, c: (0, 0)),
                  pl.BlockSpec((1, DV), lambda b, c: (0, 0))],
        out_specs=pl.BlockSpec((1, TC, H * DV), lambda b, c: (b, c, 0)),
        scratch_shapes=[pltpu.VMEM((H, DV, DK), F32)],
        compiler_params=_cp("parallel", "arbitrary"),
        name="gla",
    )(proj, proj, proj, proj, proj, proj, proj_small, w_a2, b_a.reshape(1, H * DK), norm_g.reshape(1, DV))


def _conv_ln_kernel(u_ref, halo_ref, w_ref, b_ref, g_ref, beta_ref, o_ref, cat_ref, y_ref):
    TS, HALO, CH = CONV_TS, CONV_HALO, CONV_LANE_CHUNK
    D = u_ref.shape[2]
    n_chunks = D // CH
    first = HALO - (CONV_WIDTH - 1)
    ext = TS + SUBLANES

    @pl.when(pl.program_id(1) == 0)
    def _():
        cat_ref[0:HALO, :] = jnp.zeros((HALO, D), F32)

    @pl.when(pl.program_id(1) > 0)
    def _():
        cat_ref[0:HALO, :] = halo_ref[0]

    cat_ref[HALO:HALO + TS, :] = u_ref[0]
    cat_ref[HALO + TS:HALO + ext, :] = jnp.zeros((SUBLANES, D), F32)

    def conv_chunk(c, total):
        lanes = pl.ds(pl.multiple_of(c * CH, CH), CH)
        acc = jnp.zeros((TS, CH), F32) + b_ref[:, lanes]
        for r in range(SUBLANES):
            part = None
            for a in range((first + CONV_WIDTH - 1) // SUBLANES + 1):
                k = SUBLANES * a + r - first
                if 0 <= k < CONV_WIDTH:
                    term = cat_ref[pl.ds(SUBLANES * a, ext), lanes] * w_ref[pl.ds(k, 1), lanes]
                    part = term if part is None else part + term
            acc = acc + part[r:r + TS]
        y_ref[:, lanes] = acc
        return total + acc

    total = lax.fori_loop(0, n_chunks, conv_chunk, jnp.zeros((TS, CH), F32))
    mu = jnp.broadcast_to(jnp.sum(total, axis=-1, keepdims=True) / D, (TS, CH))

    def var_chunk(c, sq):
        d = y_ref[:, pl.ds(pl.multiple_of(c * CH, CH), CH)] - mu
        return sq + d * d

    sq = lax.fori_loop(0, n_chunks, var_chunk, jnp.zeros((TS, CH), F32), unroll=2)
    inv = jnp.broadcast_to(lax.rsqrt(jnp.sum(sq, axis=-1, keepdims=True) / D + 1e-5), (TS, CH))

    def out_chunk(c, carry):
        lanes = pl.ds(pl.multiple_of(c * CH, CH), CH)
        z = (y_ref[:, lanes] - mu) * inv * g_ref[:, lanes] + beta_ref[:, lanes]
        o_ref[0, :, lanes] = _silu(z).astype(o_ref.dtype)
        return carry

    lax.fori_loop(0, n_chunks, out_chunk, 0, unroll=2)


def _conv_ln_silu(u, w_dw, b_dw, ln_g, ln_b):
    B, S, D = u.shape
    TS, HALO = CONV_TS, CONV_HALO
    ratio = TS // HALO
    vec = pl.BlockSpec((1, D), lambda b, s: (0, 0))
    return pl.pallas_call(
        _conv_ln_kernel,
        out_shape=jax.ShapeDtypeStruct((B, S, D), BF16),
        grid=(B, S // TS),
        in_specs=[pl.BlockSpec((1, TS, D), lambda b, s: (b, s, 0)),
                  pl.BlockSpec((1, HALO, D), lambda b, s: (b, jnp.maximum(s * ratio - 1, 0), 0)),
                  pl.BlockSpec((CONV_WIDTH, D), lambda b, s: (0, 0)),
                  vec, vec, vec],
        out_specs=pl.BlockSpec((1, TS, D), lambda b, s: (b, s, 0)),
        scratch_shapes=[pltpu.VMEM((HALO + TS + SUBLANES, D), F32), pltpu.VMEM((TS, D), F32)],
        compiler_params=_cp("parallel", "parallel"),
        name="conv_ln_silu",
    )(u, u, w_dw, b_dw.reshape(1, D), ln_g.reshape(1, D), ln_b.reshape(1, D))


def _hybrid_attention(h, x2, gate, j, w_in, w_out, q_norm_g, k_norm_g, cmp_pos, cmp_w1, cmp_w2,
                      gla_w_a2, gla_b_a, gla_norm_g, B, S):
    D = h.shape[1]
    n_gate = NSA_HEADS * 3
    o_gl = NSA_Q + 6 * NSA_KV
    o_gq = o_gl + n_gate
    o_ga = o_gq + 2 * GLA_HEADS * GLA_DK + GLA_HEADS * GLA_DV
    o_gr = o_ga + GLA_RANK
    w_main, w_small = _regroup_w_in(w_in, j, ((0, o_gl), (o_gq, o_ga), (o_gr, w_in.shape[2])), (o_gl, o_ga), 256)
    proj = _matmul(h, w_main, F32, 1024, 1024).reshape(B, S, -1)
    proj_small = _matmul(h, w_small, F32, 1024, 2 * LANES).reshape(B, S, 2 * LANES)

    t = jnp.arange(S, dtype=jnp.int32)
    cmp_end = jnp.arange(S // CMP_STRIDE, dtype=jnp.int32) * CMP_STRIDE + (CMP_BLOCK - 1)
    qn, ksn, vs_t, kwn, vw_t = _nsa_prep(proj, _rope_tables(t), q_norm_g, k_norm_g)
    kcmp, vcmp_t = _nsa_compress(proj, cmp_pos, cmp_w1, cmp_w2, k_norm_g, _rope_tables(cmp_end))
    gate_logits_t = proj_small[:, :, :n_gate].reshape(B, S, NSA_KV_GROUPS, 3 * NSA_HPG).transpose(0, 2, 3, 1)
    o_nsa = _nsa_attention(qn, kcmp, vcmp_t, ksn, vs_t, kwn, vw_t, gate_logits_t)
    o_gla = _gla(proj, proj_small, 1, gla_w_a2, gla_b_a, gla_norm_g, o_gl)
    xs = (o_nsa.reshape(B * S, -1), o_gla.reshape(B * S, -1))
    return _matmul_residual(xs, w_out, j, jnp.zeros((1, D), F32), x2, gate, S, 1024, 512)


def _conformer(h, x2, gate, j, w_pw1, b_pw1, w_dw, b_dw, ln_g, ln_b, w_pw2, b_pw2, B, S):
    D = h.shape[1]
    tn = 256
    b_pair = b_pw1.reshape(2, D // tn, tn).transpose(1, 0, 2).reshape(1, 2 * D)
    u = _matmul_glu_pair(h, _pair_cast(w_pw1, j, tn, 128), b_pair, F32, 1024, tn)
    v = _conv_ln_silu(u.reshape(B, S, D), w_dw, b_dw, ln_g, ln_b).reshape(B * S, D)
    return _matmul_residual((v,), w_pw2, j, b_pw2.reshape(1, D), x2, gate, S, 1024, 512)


def _swiglu(h, x2, gate, layer, w_gate, w_up, w_down, S):
    D = h.shape[1]
    act = _matmul_swiglu(h, w_gate, w_up, layer, BF16, 1024, 256)
    return _matmul_residual((act,), _cast_bf16(w_down, layer, 256), None, jnp.zeros((1, D), F32), x2, gate, S,
                            1024, 256)


def kernel(x, c, w_mod, b_mod, ada_table, norm_mix_g, norm_ffn_g, w_in, w_out, q_norm_g, k_norm_g, cmp_pos, cmp_w1, cmp_w2, gla_w_a2, gla_b_a, gla_norm_g, cv_w_pw1, cv_b_pw1, cv_w_dw, cv_b_dw, cv_ln_g, cv_ln_b, cv_w_pw2, cv_b_pw2, ffn_w_gate, ffn_w_up, ffn_w_down):
    B, S, D = x.shape
    depth = ada_table.shape[0]
    mod = _ada_mod(c, w_mod, b_mod).reshape(B, N_MOD, D)
    for layer in range(depth):
        m = mod + ada_table[layer]
        sh_a, sc_a, g_a, sh_f, sc_f, g_f = [m[:, i, :] for i in range(N_MOD)]
        h = _norm_mod(x, norm_mix_g[layer], sc_a, sh_a).reshape(B * S, D)
        x2 = x.reshape(B * S, D)
        g_a3 = g_a.reshape(B, 1, D)
        j = layer // 2
        if layer % 2 == 0:
            x2 = _hybrid_attention(h, x2, g_a3, j, w_in, w_out, q_norm_g[j], k_norm_g[j], cmp_pos[j],
                                   cmp_w1[j], cmp_w2[j], gla_w_a2[j], gla_b_a[j], gla_norm_g[j], B, S)
        else:
            x2 = _conformer(h, x2, g_a3, j, cv_w_pw1, cv_b_pw1[j], cv_w_dw[j], cv_b_dw[j], cv_ln_g[j],
                            cv_ln_b[j], cv_w_pw2, cv_b_pw2[j], B, S)
        x = x2.reshape(B, S, D)
        h = _norm_mod(x, norm_ffn_g[layer], sc_f, sh_f).reshape(B * S, D)
        x2 = _swiglu(h, x2, g_f.reshape(B, 1, D), layer, ffn_w_gate, ffn_w_up, ffn_w_down, S)
        x = x2.reshape(B, S, D)
    return x
```

```python
import functools

import numpy as np
import jax
import jax.numpy as jnp
from jax import lax
from jax.experimental import pallas as pl
from jax.experimental.pallas import tpu as pltpu

F32 = jnp.float32
BF16 = jnp.bfloat16

HEAD_DIM = 128
NSA_HEADS = 16
NSA_KV_GROUPS = 4
NSA_HPG = NSA_HEADS // NSA_KV_GROUPS
CMP_BLOCK = 32
CMP_STRIDE = 16
SLC_BLOCK = 64
SLC_TOPK = 16
WINDOW = 512
ROPE_THETA = 500000.0
ROPE_DIM = HEAD_DIM // 4
GLA_HEADS = 4
GLA_DK = 256
GLA_DV = 512
GLA_RANK = 16
GLA_TAU = 16.0
GLA_CHUNK = 64
CONV_WIDTH = 31
N_MOD = 6
NSA_Q = NSA_HEADS * HEAD_DIM
NSA_KV = NSA_KV_GROUPS * HEAD_DIM

VMEM_LIMIT_BYTES = 56 * 1024 * 1024
LANES = 128
SUBLANES = 8
NEG_BIG = -1e30
M_INIT = -1e29

ATTN_TQ = 128
ATTN_UNROLL = 4
CONV_TS = 256
CONV_HALO = 32
CONV_LANE_CHUNK = 128
GLA_TC = 128


def _cp(*sem):
    return pltpu.CompilerParams(dimension_semantics=sem, vmem_limit_bytes=VMEM_LIMIT_BYTES)


def _dot(a, b):
    return jnp.dot(a, b, preferred_element_type=F32)


def _dot_nt(a, b):
    return lax.dot_general(a, b, (((1,), (1,)), ((), ())), preferred_element_type=F32)


def _dot_tn(a, b):
    return lax.dot_general(a, b, (((0,), (0,)), ((), ())), preferred_element_type=F32)


def _silu(x):
    return x * jax.nn.sigmoid(x)


def _cast_weights_once(w_refs, wbf_refs):
    @pl.when(pl.program_id(1) == 0)
    def _():
        for w_ref, wbf_ref in zip(w_refs, wbf_refs):
            wbf_ref[...] = w_ref[...].astype(BF16)


def _mm_kernel(x_ref, w_ref, o_ref):
    o_ref[...] = _dot(x_ref[...], w_ref[...]).astype(o_ref.dtype)


def _matmul(x, w, out_dtype, tm, tn):
    M, K = x.shape
    N = w.shape[1]
    return pl.pallas_call(
        _mm_kernel,
        out_shape=jax.ShapeDtypeStruct((M, N), out_dtype),
        grid=(M // tm, N // tn),
        in_specs=[pl.BlockSpec((tm, K), lambda i, j: (i, 0)),
                  pl.BlockSpec((K, tn), lambda i, j: (0, j))],
        out_specs=pl.BlockSpec((tm, tn), lambda i, j: (i, j)),
        compiler_params=_cp("parallel", "parallel"),
        name="matmul",
    )(x, w)


def _mm_swiglu_kernel(x_ref, wg_ref, wu_ref, o_ref, wgbf_ref, wubf_ref):
    _cast_weights_once((wg_ref, wu_ref), (wgbf_ref, wubf_ref))
    x = x_ref[...]
    o_ref[...] = (_silu(_dot(x, wgbf_ref[...])) * _dot(x, wubf_ref[...])).astype(o_ref.dtype)


def _matmul_swiglu(x, w_gate, w_up, layer, out_dtype, tm, tn):
    M, K = x.shape
    N = w_gate.shape[2]
    w_spec = pl.BlockSpec((None, K, tn), lambda j, i: (layer, 0, j))
    return pl.pallas_call(
        _mm_swiglu_kernel,
        out_shape=jax.ShapeDtypeStruct((M, N), out_dtype),
        grid=(N // tn, M // tm),
        in_specs=[pl.BlockSpec((tm, K), lambda j, i: (i, 0)), w_spec, w_spec],
        out_specs=pl.BlockSpec((tm, tn), lambda j, i: (i, j)),
        scratch_shapes=[pltpu.VMEM((K, tn), BF16), pltpu.VMEM((K, tn), BF16)],
        compiler_params=_cp("arbitrary", "arbitrary"),
        name="matmul_swiglu",
    )(x, w_gate, w_up)


def _mm_glu_pair_kernel(x_ref, w_ref, b_ref, o_ref):
    tn = o_ref.shape[1]
    x = x_ref[...]
    a = _dot(x, w_ref[:, :tn]) + b_ref[:, :tn]
    b = _dot(x, w_ref[:, tn:]) + b_ref[:, tn:]
    o_ref[...] = (a * jax.nn.sigmoid(b)).astype(o_ref.dtype)


def _matmul_glu_pair(x, w_pair, b_pair, out_dtype, tm, tn):
    M, K = x.shape
    N = w_pair.shape[1] // 2
    return pl.pallas_call(
        _mm_glu_pair_kernel,
        out_shape=jax.ShapeDtypeStruct((M, N), out_dtype),
        grid=(M // tm, N // tn),
        in_specs=[pl.BlockSpec((tm, K), lambda i, j: (i, 0)),
                  pl.BlockSpec((K, 2 * tn), lambda i, j: (0, j)),
                  pl.BlockSpec((1, 2 * tn), lambda i, j: (0, j))],
        out_specs=pl.BlockSpec((tm, tn), lambda i, j: (i, j)),
        compiler_params=_cp("parallel", "parallel"),
        name="matmul_glu_pair",
    )(x, w_pair, b_pair)


def _pair_cast_kernel(w_ref, o_ref, *, tn):
    half = w_ref.shape[1] // 2
    for j in range(half // tn):
        o_ref[:, 2 * j * tn:(2 * j + 1) * tn] = w_ref[:, j * tn:(j + 1) * tn].astype(BF16)
        o_ref[:, (2 * j + 1) * tn:(2 * j + 2) * tn] = w_ref[:, half + j * tn:half + (j + 1) * tn].astype(BF16)


def _pair_cast(w, layer, tn, rows):
    _, K, N2 = w.shape
    return pl.pallas_call(
        functools.partial(_pair_cast_kernel, tn=tn),
        out_shape=jax.ShapeDtypeStruct((K, N2), BF16),
        grid=(K // rows,),
        in_specs=[pl.BlockSpec((None, rows, N2), lambda i: (layer, i, 0))],
        out_specs=pl.BlockSpec((rows, N2), lambda i: (i, 0)),
        compiler_params=_cp("parallel"),
        name="pair_cast",
    )(w)


def _mm_res_kernel(*refs, n_x, cast):
    x_refs = refs[:n_x]
    if cast:
        w_ref, b_ref, res_ref, g_ref, o_ref, wbf_ref = refs[n_x:]
        _cast_weights_once((w_ref,), (wbf_ref,))
    else:
        wbf_ref, b_ref, res_ref, g_ref, o_ref = refs[n_x:]
    y = b_ref[...]
    off = 0
    for x_ref in x_refs:
        k = x_ref.shape[1]
        y = y + _dot(x_ref[...], wbf_ref[off:off + k, :])
        off += k
    o_ref[...] = res_ref[...] + g_ref[0] * y


def _matmul_residual(xs, w, layer, bias, res, gate, rows_per_batch, tm, tn):
    M = xs[0].shape[0]
    K, N = w.shape[-2:]
    bpt = rows_per_batch // tm
    cast = w.ndim == 3
    w_block = (None, K, tn) if cast else (K, tn)
    w_index = (lambda i, j: (layer, 0, j)) if cast else (lambda i, j: (0, j))
    if cast:
        grid = (N // tn, M // tm)
        ij = lambda f: (lambda j, i: f(i, j))
        x_mode, w_mode = None, pl.Buffered(1)
        scratch = [pltpu.VMEM((K, tn), BF16)]
    else:
        grid = (M // tm, N // tn)
        ij = lambda f: f
        x_mode, w_mode = pl.Buffered(1), None
        scratch = []
    return pl.pallas_call(
        functools.partial(_mm_res_kernel, n_x=len(xs), cast=cast),
        out_shape=jax.ShapeDtypeStruct((M, N), F32),
        grid=grid,
        in_specs=[pl.BlockSpec((tm, x.shape[1]), ij(lambda i, j: (i, 0)), pipeline_mode=x_mode) for x in xs] + [
                  pl.BlockSpec(w_block, ij(w_index), pipeline_mode=w_mode),
                  pl.BlockSpec((1, tn), ij(lambda i, j: (0, j))),
                  pl.BlockSpec((tm, tn), ij(lambda i, j: (i, j))),
                  pl.BlockSpec((1, 1, tn), ij(lambda i, j: (i // bpt, 0, j)))],
        out_specs=pl.BlockSpec((tm, tn), ij(lambda i, j: (i, j))),
        scratch_shapes=scratch,
        compiler_params=_cp("arbitrary", "arbitrary"),
        name="matmul_residual",
    )(*xs, w, bias, res, gate)


def _cast_kernel(w_ref, o_ref):
    o_ref[...] = w_ref[...].astype(o_ref.dtype)


def _cast_bf16(w, layer, rows):
    _, K, N = w.shape
    return pl.pallas_call(
        _cast_kernel,
        out_shape=jax.ShapeDtypeStruct((K, N), BF16),
        grid=(K // rows,),
        in_specs=[pl.BlockSpec((None, rows, N), lambda i: (layer, i, 0))],
        out_specs=pl.BlockSpec((rows, N), lambda i: (i, 0)),
        compiler_params=_cp("parallel"),
        name="cast_bf16",
    )(w)


def _regroup_kernel(wt_ref, main_ref, small_ref, *, segments, small_starts):
    off = 0
    for lo, hi in segments:
        for r in range(lo, hi, LANES):
            main_ref[:, off:off + LANES] = wt_ref[r:r + LANES, :].T.astype(BF16)
            off += LANES
    for n, lo in enumerate(small_starts):
        small_ref[:, n * LANES:(n + 1) * LANES] = wt_ref[lo:lo + LANES, :].T.astype(BF16)


def _regroup_w_in(w_in, layer, segments, small_starts, tk):
    wt = jnp.swapaxes(w_in, 1, 2)
    _, N, K = wt.shape
    assert all((hi - lo) % LANES == 0 and lo % SUBLANES == 0 for lo, hi in segments)
    assert all(lo % SUBLANES == 0 and lo + LANES <= N for lo in small_starts)
    n_main = sum(hi - lo for lo, hi in segments)
    n_small = LANES * len(small_starts)
    return pl.pallas_call(
        functools.partial(_regroup_kernel, segments=segments, small_starts=small_starts),
        out_shape=(jax.ShapeDtypeStruct((K, n_main), BF16), jax.ShapeDtypeStruct((K, n_small), BF16)),
        grid=(K // tk,),
        in_specs=[pl.BlockSpec((None, N, tk), lambda k: (layer, 0, k))],
        out_specs=(pl.BlockSpec((tk, n_main), lambda k: (k, 0)), pl.BlockSpec((tk, n_small), lambda k: (k, 0))),
        compiler_params=_cp("parallel"),
        name="regroup_w_in",
    )(wt)


def _mod_kernel(c_ref, w_ref, b_ref, o_ref):
    a = _silu(c_ref[...]).astype(BF16)
    o_ref[...] = _dot(a, w_ref[...].astype(BF16)) + b_ref[...]


def _ada_mod(c, w_mod, b_mod):
    B, D = c.shape
    N = w_mod.shape[1]
    tn = 512
    return pl.pallas_call(
        _mod_kernel,
        out_shape=jax.ShapeDtypeStruct((B, N), F32),
        grid=(N // tn,),
        in_specs=[pl.BlockSpec((B, D), lambda j: (0, 0)),
                  pl.BlockSpec((D, tn), lambda j: (0, j)),
                  pl.BlockSpec((1, tn), lambda j: (0, j))],
        out_specs=pl.BlockSpec((B, tn), lambda j: (0, j)),
        compiler_params=_cp("parallel"),
        name="ada_mod",
    )(c, w_mod, b_mod.reshape(1, N))


def _norm_mod_kernel(x_ref, g_ref, sc_ref, sh_ref, o_ref, gm_ref):
    rows_per_trip = 2 * SUBLANES
    gm_ref[...] = g_ref[...] * (1.0 + sc_ref[0])

    def trip(i, carry):
        rows = pl.ds(pl.multiple_of(i * rows_per_trip, rows_per_trip), rows_per_trip)
        x = x_ref[0, rows, :]
        r = lax.rsqrt(jnp.mean(x * x, axis=-1, keepdims=True) + 1e-6)
        o_ref[0, rows, :] = (x * r * gm_ref[...] + sh_ref[0]).astype(o_ref.dtype)
        return carry

    lax.fori_loop(0, x_ref.shape[1] // rows_per_trip, trip, 0, unroll=4)


def _norm_mod(x, g, sc, sh):
    B, S, D = x.shape
    ts = 512
    return pl.pallas_call(
        _norm_mod_kernel,
        out_shape=jax.ShapeDtypeStruct((B, S, D), BF16),
        grid=(B, S // ts),
        in_specs=[pl.BlockSpec((1, ts, D), lambda b, s: (b, s, 0)),
                  pl.BlockSpec((1, D), lambda b, s: (0, 0)),
                  pl.BlockSpec((1, 1, D), lambda b, s: (b, 0, 0)),
                  pl.BlockSpec((1, 1, D), lambda b, s: (b, 0, 0))],
        out_specs=pl.BlockSpec((1, ts, D), lambda b, s: (b, s, 0)),
        scratch_shapes=[pltpu.VMEM((1, D), F32)],
        compiler_params=_cp("parallel", "parallel"),
        name="norm_mod",
    )(x, g.reshape(1, D), sc.reshape(B, 1, D), sh.reshape(B, 1, D))


def _rope_tables(pos):
    half = ROPE_DIM // 2
    inv_freq = ROPE_THETA ** (-jnp.arange(half, dtype=F32) / half)
    ang = pos.astype(F32)[:, None] * inv_freq[None, :]
    cos, sin = jnp.cos(ang), jnp.sin(ang)
    n = pos.shape[0]
    rest = HEAD_DIM - ROPE_DIM
    c = jnp.concatenate([cos, cos, jnp.ones((n, rest), F32)], axis=-1)
    s_lo = jnp.concatenate([-sin, jnp.zeros((n, HEAD_DIM - half), F32)], axis=-1)
    s_hi = jnp.concatenate([jnp.zeros((n, half), F32), sin, jnp.zeros((n, rest), F32)], axis=-1)
    return c, s_lo, s_hi


def _norm_rope(x, g, c, s_lo, s_hi):
    y = x * lax.rsqrt(jnp.mean(x * x, axis=-1, keepdims=True) + 1e-6) * g
    half = ROPE_DIM // 2
    return y * c + pltpu.roll(y, HEAD_DIM - half, 1) * s_lo + pltpu.roll(y, half, 1) * s_hi


def _nsa_prep_kernel(q_ref, ks_ref, vs_ref, kw_ref, vw_ref, c_ref, slo_ref, shi_ref, qg_ref, kg_ref,
                     qo_ref, kso_ref, vso_ref, kwo_ref, vwo_ref):
    c, s_lo, s_hi = c_ref[...], slo_ref[...], shi_ref[...]
    scale = HEAD_DIM ** -0.5
    for h in range(NSA_HEADS):
        sl = slice(h * HEAD_DIM, (h + 1) * HEAD_DIM)
        qo_ref[0, :, sl] = (_norm_rope(q_ref[0, :, sl], qg_ref[...], c, s_lo, s_hi) * scale).astype(BF16)
    for g in range(NSA_KV_GROUPS):
        sl = slice(g * HEAD_DIM, (g + 1) * HEAD_DIM)
        kso_ref[0, :, sl] = _norm_rope(ks_ref[0, :, sl], kg_ref[1:2, :], c, s_lo, s_hi).astype(BF16)
        kwo_ref[0, :, sl] = _norm_rope(kw_ref[0, :, sl], kg_ref[2:3, :], c, s_lo, s_hi).astype(BF16)
    vso_ref[0] = vs_ref[0].T.astype(BF16)
    vwo_ref[0] = vw_ref[0].T.astype(BF16)


def _nsa_prep(proj, tables, q_norm_g, k_norm_g):
    B, S, _ = proj.shape
    ts = 256
    kvb = NSA_Q // NSA_KV

    def kv_spec(n):
        return pl.BlockSpec((1, ts, NSA_KV), lambda b, s: (b, s, kvb + n))

    tab = pl.BlockSpec((ts, HEAD_DIM), lambda b, s: (s, 0))
    out_k = pl.BlockSpec((1, ts, NSA_KV), lambda b, s: (b, s, 0))
    out_vt = pl.BlockSpec((1, NSA_KV, ts), lambda b, s: (b, 0, s))
    k_shape = jax.ShapeDtypeStruct((B, S, NSA_KV), BF16)
    vt_shape = jax.ShapeDtypeStruct((B, NSA_KV, S), BF16)
    return pl.pallas_call(
        _nsa_prep_kernel,
        out_shape=(jax.ShapeDtypeStruct((B, S, NSA_Q), BF16), k_shape, vt_shape, k_shape, vt_shape),
        grid=(B, S // ts),
        in_specs=[pl.BlockSpec((1, ts, NSA_Q), lambda b, s: (b, s, 0)),
                  kv_spec(2), kv_spec(3), kv_spec(4), kv_spec(5), tab, tab, tab,
                  pl.BlockSpec((1, HEAD_DIM), lambda b, s: (0, 0)),
                  pl.BlockSpec((3, HEAD_DIM), lambda b, s: (0, 0))],
        out_specs=(pl.BlockSpec((1, ts, NSA_Q), lambda b, s: (b, s, 0)), out_k, out_vt, out_k, out_vt),
        compiler_params=_cp("parallel", "parallel"),
        name="nsa_prep",
    )(proj, proj, proj, proj, proj, *tables, q_norm_g.reshape(1, HEAD_DIM), k_norm_g)


def _compress_kernel(kc_ref, vc_ref, pos_ref, w1_ref, w2_ref, kg_ref, c_ref, slo_ref, shi_ref,
                     ko_ref, vo_ref):
    half_blk = CMP_BLOCK // 2
    n_seg = kc_ref.shape[1] // CMP_STRIDE

    def compress(tok_ref, j):
        u = jnp.zeros((n_seg, w1_ref.shape[2]), F32)
        v = jnp.zeros((n_seg, w1_ref.shape[2]), F32)
        for l in range(half_blk):
            x = tok_ref[0, pl.ds(l, n_seg, stride=CMP_STRIDE), :]
            xa = (x + pos_ref[j, l:l + 1, :]).astype(BF16)
            xb = (x + pos_ref[j, half_blk + l:half_blk + l + 1, :]).astype(BF16)
            u = u + _dot(xa, w1_ref[j, l * HEAD_DIM:(l + 1) * HEAD_DIM, :])
            v = v + _dot(xb, w1_ref[j, (half_blk + l) * HEAD_DIM:(half_blk + l + 1) * HEAD_DIM, :])
        h = u + pltpu.roll(v, n_seg - 1, 0)
        return _dot(jax.nn.gelu(h).astype(BF16), w2_ref[j])

    k = compress(kc_ref, 0)
    ko_ref[0, 0] = _norm_rope(k, kg_ref[0:1, :], c_ref[...], slo_ref[...], shi_ref[...]).astype(BF16)
    vo_ref[0, 0] = compress(vc_ref, 1).T.astype(BF16)


def _nsa_compress(proj, cmp_pos, cmp_w1, cmp_w2, k_norm_g, cmp_tables):
    B, S, _ = proj.shape
    G = NSA_KV_GROUPS
    n_seg = S // CMP_STRIDE
    kcb = NSA_Q // HEAD_DIM
    full2 = lambda b, g: (0, 0)
    full3 = lambda b, g: (0, 0, 0)
    return pl.pallas_call(
        _compress_kernel,
        out_shape=(jax.ShapeDtypeStruct((B, G, n_seg, HEAD_DIM), BF16),
                   jax.ShapeDtypeStruct((B, G, HEAD_DIM, n_seg), BF16)),
        grid=(B, G),
        in_specs=[pl.BlockSpec((1, S, HEAD_DIM), lambda b, g: (b, 0, kcb + g)),
                  pl.BlockSpec((1, S, HEAD_DIM), lambda b, g: (b, 0, kcb + G + g)),
                  pl.BlockSpec(cmp_pos.shape, full3),
                  pl.BlockSpec(cmp_w1.shape, full3),
                  pl.BlockSpec(cmp_w2.shape, full3),
                  pl.BlockSpec((3, HEAD_DIM), full2),
                  pl.BlockSpec((n_seg, HEAD_DIM), full2),
                  pl.BlockSpec((n_seg, HEAD_DIM), full2),
                  pl.BlockSpec((n_seg, HEAD_DIM), full2)],
        out_specs=(pl.BlockSpec((1, 1, n_seg, HEAD_DIM), lambda b, g: (b, g, 0, 0)),
                   pl.BlockSpec((1, 1, HEAD_DIM, n_seg), lambda b, g: (b, g, 0, 0))),
        compiler_params=_cp("parallel", "parallel"),
        name="nsa_compress",
    )(proj, proj, cmp_pos, cmp_w1.astype(BF16), cmp_w2.astype(BF16), k_norm_g, *cmp_tables)


def _fold_rows(x, op):
    return op(x.reshape(x.shape[0] // SUBLANES, SUBLANES, x.shape[1]), axis=0)


def _nsa_attn_kernel(q_ref, kc_ref, vct_ref, ks_ref, vst_ref, kw_ref, vwt_ref, gate_ref, ovl_ref, exp_ref,
                     o_ref, selm_ref, ss_ref, sw_ref):
    TQ, TK = ATTN_TQ, LANES
    R = NSA_HPG * TQ
    qi = pl.program_id(2)
    t0 = qi * TQ
    q = jnp.concatenate([q_ref[0, :, h * HEAD_DIM:(h + 1) * HEAD_DIM] for h in range(NSA_HPG)], axis=0)
    tq = t0 + (lax.broadcasted_iota(jnp.int32, (TK, R), 1) & (TQ - 1))
    key = lax.broadcasted_iota(jnp.int32, (TK, R), 0)

    def score_tile(k_ref, scr_ref, slot, kt, m8, use_sel=False, causal=False, window=False, valid=None):
        off = pl.multiple_of(kt * TK, TK)
        s = _dot_nt(k_ref[0, pl.ds(off, TK), :], q)
        mask = None
        if use_sel:
            sm = selm_ref[pl.ds(off, TK), :]
            mask = jnp.concatenate([sm] * NSA_HPG, axis=1) > 0.5
        if causal:
            c = (off + key) <= tq
            mask = c if mask is None else mask & c
        if window:
            w = (tq - (off + key)) < WINDOW
            mask = w if mask is None else mask & w
        if valid is not None:
            mask = valid if mask is None else mask & valid
        if mask is not None:
            s = jnp.where(mask, s, NEG_BIG)
        scr_ref[slot] = s
        return jnp.maximum(m8, _fold_rows(s, jnp.max))

    def value_tile(scr_ref, vt_ref, slot, kt, m, carry):
        l8, acc = carry
        off = pl.multiple_of(kt * TK, TK)
        p = jnp.exp(scr_ref[slot] - m)
        return l8 + _fold_rows(p, jnp.sum), acc + _dot(vt_ref[0, :, pl.ds(off, TK)], p.astype(BF16))

    def finish(carry):
        l8, acc = carry
        l = jnp.sum(l8, axis=0, keepdims=True)
        return acc / jnp.where(l > 0.0, l, 1.0)

    m_init = jnp.full((SUBLANES, R), M_INIT, F32)
    acc_init = (jnp.zeros((SUBLANES, R), F32), jnp.zeros((HEAD_DIM, R), F32))

    n_back = WINDOW // TK
    win_tiles = [(jnp.maximum(qi - n_back + u, 0), qi - n_back + u >= 0) for u in range(n_back)]
    m8_w = m_init
    for u, (kt, valid) in enumerate(win_tiles):
        m8_w = score_tile(kw_ref, sw_ref, u, kt, m8_w, window=(u == 0), valid=valid)
    m8_w = score_tile(kw_ref, sw_ref, n_back, qi, m8_w, causal=True)
    m_w = jnp.max(m8_w, axis=0, keepdims=True)

    s = _dot_nt(kc_ref[0, 0], q)
    mask = (key * CMP_STRIDE + (CMP_BLOCK - 1)) <= tq
    s = jnp.where(mask, s, NEG_BIG)
    p = jnp.where(mask, jnp.exp(s - jnp.max(s, axis=0, keepdims=True)), 0.0)
    l = jnp.sum(p, axis=0, keepdims=True)
    pb = (p / jnp.where(l > 0.0, l, 1.0)).astype(BF16)
    o_cmp = _dot(vct_ref[0, 0], pb)

    c_w = acc_init
    for u, (kt, _) in enumerate(win_tiles):
        c_w = value_tile(sw_ref, vwt_ref, u, kt, m_w, c_w)
    o_win = finish(value_tile(sw_ref, vwt_ref, n_back, qi, m_w, c_w))

    n_slc = ovl_ref.shape[0]
    ranked = t0 + TQ > SLC_TOPK * SLC_BLOCK
    last_group = qi // ATTN_UNROLL
    group_keys = ATTN_UNROLL * TK

    @pl.when(ranked)
    def _():
        imp_heads = _dot(ovl_ref[...], pb)
        imp = imp_heads[:, 0:TQ]
        for h in range(1, NSA_HPG):
            imp = imp + imp_heads[:, h * TQ:(h + 1) * TQ]
        t = t0 + lax.broadcasted_iota(jnp.int32, (n_slc, TQ), 1)
        blk = lax.broadcasted_iota(jnp.int32, (n_slc, TQ), 0)
        cur = t // SLC_BLOCK
        forced = (blk == 0) | (blk == cur) | (blk == cur - 1)
        valid = blk * SLC_BLOCK <= t
        val = jnp.where(forced, jnp.inf, jnp.where(valid, imp, -jnp.inf))
        rank = jnp.zeros((n_slc, TQ), F32)
        for i in range(n_slc):
            vi = val[i:i + 1, :]
            ahead = (vi > val) | ((vi == val) & (blk > i))
            rank = rank + jnp.where(ahead, 1.0, 0.0)
        sel = jnp.where((rank < float(SLC_TOPK)) & (val > -jnp.inf), 1.0, 0.0).astype(BF16)
        selm_ref[...] = _dot(exp_ref[...], sel)

    @pl.when(jnp.logical_not(ranked))
    def _():
        rows = pl.ds(pl.multiple_of(last_group * group_keys, group_keys), group_keys)
        selm_ref[rows, :] = jnp.ones((group_keys, TQ), F32)

    def score_group(a, m8, use_sel, causal):
        for u in range(ATTN_UNROLL):
            kt = a * ATTN_UNROLL + u
            m8 = score_tile(ks_ref, ss_ref, kt, kt, m8, use_sel=use_sel, causal=causal)
        return m8

    n_masked = jnp.where(ranked, last_group, 0)
    m8 = lax.fori_loop(0, n_masked, lambda a, m: score_group(a, m, True, False), m_init)
    m8 = lax.fori_loop(0, last_group - n_masked, lambda a, m: score_group(a, m, False, False), m8)
    m8_s = score_group(last_group, m8, True, True)
    m_s = jnp.max(m8_s, axis=0, keepdims=True)

    def value_group(a, carry):
        for u in range(ATTN_UNROLL):
            kt = a * ATTN_UNROLL + u
            carry = value_tile(ss_ref, vst_ref, kt, kt, m_s, carry)
        return carry

    o_slc = finish(lax.fori_loop(0, last_group + 1, value_group, acc_init))

    gate = jax.nn.sigmoid(gate_ref[0, 0])
    for h in range(NSA_HPG):
        cols = slice(h * TQ, (h + 1) * TQ)
        o = (gate[3 * h:3 * h + 1, :] * o_cmp[:, cols] + gate[3 * h + 1:3 * h + 2, :] * o_slc[:, cols]
             + gate[3 * h + 2:3 * h + 3, :] * o_win[:, cols])
        o_ref[0, :, h * HEAD_DIM:(h + 1) * HEAD_DIM] = o.T.astype(o_ref.dtype)


def _nsa_attention(qn, kcmp, vcmp_t, ksn, vs_t, kwn, vw_t, gate_logits_t):
    B, S, _ = qn.shape
    G = NSA_KV_GROUPS
    TQ = ATTN_TQ
    n_cmp = S // CMP_STRIDE
    n_slc = S // SLC_BLOCK
    assert n_cmp == LANES and WINDOW % LANES == 0 and TQ == LANES
    cmp_start = np.arange(n_cmp) * CMP_STRIDE
    slc_start = np.arange(n_slc) * SLC_BLOCK
    overlap = np.clip(np.minimum(cmp_start[None, :] + CMP_BLOCK, slc_start[:, None] + SLC_BLOCK)
                      - np.maximum(cmp_start[None, :], slc_start[:, None]), 0, None) / CMP_STRIDE
    ovl = jnp.asarray(overlap, dtype=BF16)
    expand = jnp.asarray((np.arange(S)[:, None] // SLC_BLOCK) == np.arange(n_slc)[None, :], dtype=BF16)
    gq = NSA_HPG * HEAD_DIM
    q_spec = pl.BlockSpec((1, TQ, gq), lambda b, g, i: (b, i, g))
    k_spec = pl.BlockSpec((1, S, HEAD_DIM), lambda b, g, i: (b, 0, g))
    vt_spec = pl.BlockSpec((1, HEAD_DIM, S), lambda b, g, i: (b, g, 0))
    return pl.pallas_call(
        _nsa_attn_kernel,
        out_shape=jax.ShapeDtypeStruct((B, S, NSA_Q), BF16),
        grid=(B, G, S // TQ),
        in_specs=[q_spec,
                  pl.BlockSpec((1, 1, n_cmp, HEAD_DIM), lambda b, g, i: (b, g, 0, 0)),
                  pl.BlockSpec((1, 1, HEAD_DIM, n_cmp), lambda b, g, i: (b, g, 0, 0)),
                  k_spec, vt_spec, k_spec, vt_spec,
                  pl.BlockSpec((1, 1, 3 * NSA_HPG, TQ), lambda b, g, i: (b, g, 0, i)),
                  pl.BlockSpec(ovl.shape, lambda b, g, i: (0, 0)),
                  pl.BlockSpec(expand.shape, lambda b, g, i: (0, 0))],
        out_specs=q_spec,
        scratch_shapes=[pltpu.VMEM((S, TQ), F32),
                        pltpu.VMEM((S // LANES, LANES, NSA_HPG * TQ), F32),
                        pltpu.VMEM((WINDOW // LANES + 1, LANES, NSA_HPG * TQ), F32)],
        compiler_params=_cp("parallel", "parallel", "parallel"),
        name="nsa_attention",
    )(qn, kcmp, vcmp_t, ksn, vs_t, kwn, vw_t, gate_logits_t, ovl, expand)


def _gla_kernel(q_ref, k_ref, v0_ref, v1_ref, r0_ref, r1_ref, a_ref, wa_ref, ba_ref, ng_ref, o_ref, state_ref):
    C, DK, DV = GLA_CHUNK, GLA_DK, GLA_DV
    half = GLA_HEADS // 2
    v_refs, r_refs = (v0_ref, v1_ref), (r0_ref, r1_ref)

    @pl.when(pl.program_id(1) == 0)
    def _():
        state_ref[...] = jnp.zeros_like(state_ref)

    row = lax.broadcasted_iota(jnp.int32, (C, C), 0)
    colc = lax.broadcasted_iota(jnp.int32, (C, C), 1)
    causal = colc <= row
    tri = jnp.where(causal, 1.0, 0.0).astype(BF16)
    wa = wa_ref[...].astype(BF16)
    heads = range(GLA_HEADS)
    for c in range(q_ref.shape[1] // C):
        rows = slice(c * C, (c + 1) * C)
        a_low = a_ref[0, rows, 0:GLA_RANK].astype(BF16)
        dk = [slice(h * DK, (h + 1) * DK) for h in heads]
        dv = [slice(h * DV, (h + 1) * DV) for h in heads]
        dvh = [slice((h % half) * DV, (h % half + 1) * DV) for h in heads]
        z = [_dot(a_low, wa[:, dk[h]]) + ba_ref[:, dk[h]] for h in heads]
        la = [(jnp.minimum(z[h], 0.0) - jnp.log1p(jnp.exp(-jnp.abs(z[h])))) / GLA_TAU for h in heads]
        hi = [la[h].astype(BF16) for h in heads]
        r1 = [la[h] - hi[h].astype(F32) for h in heads]
        mid = [r1[h].astype(BF16) for h in heads]
        lo = [(r1[h] - mid[h].astype(F32)).astype(BF16) for h in heads]
        bcum = [_dot(tri, hi[h]) + _dot(tri, mid[h]) + _dot(tri, lo[h]) for h in heads]
        blast = [bcum[h][C - 1:C, :] for h in heads]
        kh = [k_ref[0, rows, dk[h]] for h in heads]
        q_in = [(q_ref[0, rows, dk[h]] * (DK ** -0.5) * jnp.exp(bcum[h])).astype(BF16) for h in heads]
        k_in = [(kh[h] * jnp.exp(-bcum[h])).astype(BF16) for h in heads]
        k_out = [(kh[h] * jnp.exp(blast[h] - bcum[h])).astype(BF16) for h in heads]
        vb = [v_refs[h // half][0, rows, dvh[h]].astype(BF16) for h in heads]
        a_intra = [jnp.where(causal, _dot_nt(q_in[h], k_in[h]), 0.0).astype(BF16) for h in heads]
        state = [state_ref[h] for h in heads]
        o = [_dot(a_intra[h], vb[h]) + _dot_nt(q_in[h], state[h].astype(BF16)) for h in heads]
        upd = [_dot_tn(vb[h], k_out[h]) for h in heads]
        for h in heads:
            state_ref[h] = state[h] * jnp.exp(blast[h]) + upd[h]
            y = o[h] * lax.rsqrt(jnp.mean(o[h] * o[h], axis=-1, keepdims=True) + 1e-6) * ng_ref[...]
            o_ref[0, rows, dv[h]] = (y * _silu(r_refs[h // half][0, rows, dvh[h]])).astype(o_ref.dtype)


def _gla(proj, proj_small, a_block, w_a2, b_a, norm_g, q_col):
    B, S, _ = proj.shape
    H, DK, DV, TC = GLA_HEADS, GLA_DK, GLA_DV, GLA_TC
    hv = H * DV // 2
    qb = q_col // (H * DK)
    vb = (q_col + 2 * H * DK) // hv
    assert q_col % (H * DK) == 0 and (q_col + 2 * H * DK) % hv == 0
    return pl.pallas_call(
        _gla_kernel,
        out_shape=jax.ShapeDtypeStruct((B, S, H * DV), BF16),
        grid=(B, S // TC),
        in_specs=[pl.BlockSpec((1, TC, H * DK), lambda b, c: (b, c, qb)),
                  pl.BlockSpec((1, TC, H * DK), lambda b, c: (b, c, qb + 1)),
                  pl.BlockSpec((1, TC, hv), lambda b, c: (b, c, vb)),
                  pl.BlockSpec((1, TC, hv), lambda b, c: (b, c, vb + 1)),
                  pl.BlockSpec((1, TC, hv), lambda b, c: (b, c, vb + 2)),
                  pl.BlockSpec((1, TC, hv), lambda b, c: (b, c, vb + 3)),
                  pl.BlockSpec((1, TC, LANES), lambda b, c: (b, c, a_block)),
                  pl.BlockSpec((GLA_RANK, H * DK), lambda b, c: (0, 0)),
                  pl.BlockSpec((1, H * DK), lambda b, c: (0, 0)),
                  pl.BlockSpec((1, DV), lambda b, c: (0, 0))],
        out_specs=pl.BlockSpec((1, TC, H * DV), lambda b, c: (b, c, 0)),
        scratch_shapes=[pltpu.VMEM((H, DV, DK), F32)],
        compiler_params=_cp("parallel", "arbitrary"),
        name="gla",
    )(proj, proj, proj, proj, proj, proj, proj_small, w_a2, b_a.reshape(1, H * DK), norm_g.reshape(1, DV))


def _conv_ln_kernel(u_ref, halo_ref, w_ref, b_ref, g_ref, beta_ref, o_ref, cat_ref, y_ref):
    TS, HALO, CH = CONV_TS, CONV_HALO, CONV_LANE_CHUNK
    D = u_ref.shape[2]
    n_chunks = D // CH
    first = HALO - (CONV_WIDTH - 1)
    ext = TS + SUBLANES

    @pl.when(pl.program_id(1) == 0)
    def _():
        cat_ref[0:HALO, :] = jnp.zeros((HALO, D), F32)

    @pl.when(pl.program_id(1) > 0)
    def _():
        cat_ref[0:HALO, :] = halo_ref[0]

    cat_ref[HALO:HALO + TS, :] = u_ref[0]
    cat_ref[HALO + TS:HALO + ext, :] = jnp.zeros((SUBLANES, D), F32)

    def conv_chunk(c, total):
        lanes = pl.ds(pl.multiple_of(c * CH, CH), CH)
        acc = jnp.zeros((TS, CH), F32) + b_ref[:, lanes]
        for r in range(SUBLANES):
            part = None
            for a in range((first + CONV_WIDTH - 1) // SUBLANES + 1):
                k = SUBLANES * a + r - first
                if 0 <= k < CONV_WIDTH:
                    term = cat_ref[pl.ds(SUBLANES * a, ext), lanes] * w_ref[pl.ds(k, 1), lanes]
                    part = term if part is None else part + term
            acc = acc + part[r:r + TS]
        y_ref[:, lanes] = acc
        return total + acc

    total = lax.fori_loop(0, n_chunks, conv_chunk, jnp.zeros((TS, CH), F32))
    mu = jnp.broadcast_to(jnp.sum(total, axis=-1, keepdims=True) / D, (TS, CH))

    def var_chunk(c, sq):
        d = y_ref[:, pl.ds(pl.multiple_of(c * CH, CH), CH)] - mu
        return sq + d * d

    sq = lax.fori_loop(0, n_chunks, var_chunk, jnp.zeros((TS, CH), F32), unroll=2)
    inv = jnp.broadcast_to(lax.rsqrt(jnp.sum(sq, axis=-1, keepdims=True) / D + 1e-5), (TS, CH))

    def out_chunk(c, carry):
        lanes = pl.ds(pl.multiple_of(c * CH, CH), CH)
        z = (y_ref[:, lanes] - mu) * inv * g_ref[:, lanes] + beta_ref[:, lanes]
        o_ref[0, :, lanes] = _silu(z).astype(o_ref.dtype)
        return carry

    lax.fori_loop(0, n_chunks, out_chunk, 0, unroll=2)


def _conv_ln_silu(u, w_dw, b_dw, ln_g, ln_b):
    B, S, D = u.shape
    TS, HALO = CONV_TS, CONV_HALO
    ratio = TS // HALO
    vec = pl.BlockSpec((1, D), lambda b, s: (0, 0))
    return pl.pallas_call(
        _conv_ln_kernel,
        out_shape=jax.ShapeDtypeStruct((B, S, D), BF16),
        grid=(B, S // TS),
        in_specs=[pl.BlockSpec((1, TS, D), lambda b, s: (b, s, 0)),
                  pl.BlockSpec((1, HALO, D), lambda b, s: (b, jnp.maximum(s * ratio - 1, 0), 0)),
                  pl.BlockSpec((CONV_WIDTH, D), lambda b, s: (0, 0)),
                  vec, vec, vec],
        out_specs=pl.BlockSpec((1, TS, D), lambda b, s: (b, s, 0)),
        scratch_shapes=[pltpu.VMEM((HALO + TS + SUBLANES, D), F32), pltpu.VMEM((TS, D), F32)],
        compiler_params=_cp("parallel", "parallel"),
        name="conv_ln_silu",
    )(u, u, w_dw, b_dw.reshape(1, D), ln_g.reshape(1, D), ln_b.reshape(1, D))


def _hybrid_attention(h, x2, gate, j, w_in, w_out, q_norm_g, k_norm_g, cmp_pos, cmp_w1, cmp_w2,
                      gla_w_a2, gla_b_a, gla_norm_g, B, S):
    D = h.shape[1]
    n_gate = NSA_HEADS * 3
    o_gl = NSA_Q + 6 * NSA_KV
    o_gq = o_gl + n_gate
    o_ga = o_gq + 2 * GLA_HEADS * GLA_DK + GLA_HEADS * GLA_DV
    o_gr = o_ga + GLA_RANK
    w_main, w_small = _regroup_w_in(w_in, j, ((0, o_gl), (o_gq, o_ga), (o_gr, w_in.shape[2])), (o_gl, o_ga), 256)
    proj = _matmul(h, w_main, F32, 1024, 1024).reshape(B, S, -1)
    proj_small = _matmul(h, w_small, F32, 1024, 2 * LANES).reshape(B, S, 2 * LANES)

    t = jnp.arange(S, dtype=jnp.int32)
    cmp_end = jnp.arange(S // CMP_STRIDE, dtype=jnp.int32) * CMP_STRIDE + (CMP_BLOCK - 1)
    qn, ksn, vs_t, kwn, vw_t = _nsa_prep(proj, _rope_tables(t), q_norm_g, k_norm_g)
    kcmp, vcmp_t = _nsa_compress(proj, cmp_pos, cmp_w1, cmp_w2, k_norm_g, _rope_tables(cmp_end))
    gate_logits_t = proj_small[:, :, :n_gate].reshape(B, S, NSA_KV_GROUPS, 3 * NSA_HPG).transpose(0, 2, 3, 1)
    o_nsa = _nsa_attention(qn, kcmp, vcmp_t, ksn, vs_t, kwn, vw_t, gate_logits_t)
    o_gla = _gla(proj, proj_small, 1, gla_w_a2, gla_b_a, gla_norm_g, o_gl)
    xs = (o_nsa.reshape(B * S, -1), o_gla.reshape(B * S, -1))
    return _matmul_residual(xs, w_out, j, jnp.zeros((1, D), F32), x2, gate, S, 1024, 512)


def _conformer(h, x2, gate, j, w_pw1, b_pw1, w_dw, b_dw, ln_g, ln_b, w_pw2, b_pw2, B, S):
    D = h.shape[1]
    tn = 256
    b_pair = b_pw1.reshape(2, D // tn, tn).transpose(1, 0, 2).reshape(1, 2 * D)
    u = _matmul_glu_pair(h, _pair_cast(w_pw1, j, tn, 128), b_pair, F32, 1024, tn)
    v = _conv_ln_silu(u.reshape(B, S, D), w_dw, b_dw, ln_g, ln_b).reshape(B * S, D)
    return _matmul_residual((v,), w_pw2, j, b_pw2.reshape(1, D), x2, gate, S, 1024, 512)


def _swiglu(h, x2, gate, layer, w_gate, w_up, w_down, S):
    D = h.shape[1]
    act = _matmul_swiglu(h, w_gate, w_up, layer, BF16, 1024, 256)
    return _matmul_residual((act,), _cast_bf16(w_down, layer, 256), None, jnp.zeros((1, D), F32), x2, gate, S,
                            1024, 512)


def kernel(x, c, w_mod, b_mod, ada_table, norm_mix_g, norm_ffn_g, w_in, w_out, q_norm_g, k_norm_g, cmp_pos, cmp_w1, cmp_w2, gla_w_a2, gla_b_a, gla_norm_g, cv_w_pw1, cv_b_pw1, cv_w_dw, cv_b_dw, cv_ln_g, cv_ln_b, cv_w_pw2, cv_b_pw2, ffn_w_gate, ffn_w_up, ffn_w_down):
    B, S, D = x.shape
    depth = ada_table.shape[0]
    mod = _ada_mod(c, w_mod, b_mod).reshape(B, N_MOD, D)
    for layer in range(depth):
        m = mod + ada_table[layer]
        sh_a, sc_a, g_a, sh_f, sc_f, g_f = [m[:, i, :] for i in range(N_MOD)]
        h = _norm_mod(x, norm_mix_g[layer], sc_a, sh_a).reshape(B * S, D)
        x2 = x.reshape(B * S, D)
        g_a3 = g_a.reshape(B, 1, D)
        j = layer // 2
        if layer % 2 == 0:
            x2 = _hybrid_attention(h, x2, g_a3, j, w_in, w_out, q_norm_g[j], k_norm_g[j], cmp_pos[j],
                                   cmp_w1[j], cmp_w2[j], gla_w_a2[j], gla_b_a[j], gla_norm_g[j], B, S)
        else:
            x2 = _conformer(h, x2, g_a3, j, cv_w_pw1, cv_b_pw1[j], cv_w_dw[j], cv_b_dw[j], cv_ln_g[j],
                            cv_ln_b[j], cv_w_pw2, cv_b_pw2[j], B, S)
        x = x2.reshape(B, S, D)
        h = _norm_mod(x, norm_ffn_g[layer], sc_f, sh_f).reshape(B * S, D)
        x2 = _swiglu(h, x2, g_f.reshape(B, 1, D), layer, ffn_w_gate, ffn_w_up, ffn_w_down, S)
        x = x2.reshape(B, S, D)
    return x
```
